```python
import math
import jax
import jax.numpy as jnp
from jax import lax
import numpy as np

D_MODEL = 2048
BATCH = 1
SEQ = 8192
DEPTH = 4

N_MIXERS = 2
A_HEADS = 16
A_HEAD_DIM = D_MODEL // A_HEADS
A_KV_HEADS = 4
A_GROUP = A_HEADS // A_KV_HEADS
IDX_HEADS = 16
IDX_DIM = 64
TOPK_MAX = 256
Q_BLOCK = 128
REL_BUCKETS = 32
REL_MAX_DIST = 128
B_KEY_DIM = 128
B_HEADS = D_MODEL // B_KEY_DIM
B_VAL_DIM = D_MODEL // B_HEADS
CHUNK = 64
D_FF = 5632
EPS = 1e-6

N_A_LAYERS = (DEPTH + N_MIXERS - 1) // N_MIXERS
N_B_LAYERS = DEPTH // N_MIXERS
A_Q_COLS = A_HEADS * A_HEAD_DIM
A_KV_COLS = A_KV_HEADS * A_HEAD_DIM
A_IQ_COLS = IDX_HEADS * IDX_DIM
A_IN_COLS = A_Q_COLS + 2 * A_KV_COLS + A_IQ_COLS + IDX_DIM + IDX_HEADS
B_IN_COLS = 4 * D_MODEL

kernel_name = 'hybrid_dsa_hgrn2_macaron_trunk'


def rmsnorm(x, g):
    xf = x.astype(jnp.float32)
    y = xf * lax.rsqrt(jnp.mean(xf * xf, axis=-1, keepdims=True) + EPS)
    return (y * g.astype(jnp.float32)).astype(x.dtype)


def swiglu(x, w_gate, w_up, w_down):
    return (jax.nn.silu(x @ w_gate) * (x @ w_up)) @ w_down


def t5_bucket(dist):
    exact = REL_BUCKETS // 2
    d = jnp.maximum(dist, 0)
    df = jnp.maximum(d, 1).astype(jnp.float32)
    large = exact + (jnp.log(df / exact) / math.log(REL_MAX_DIST / exact) * (REL_BUCKETS - exact)).astype(jnp.int32)
    large = jnp.minimum(large, REL_BUCKETS - 1)
    return jnp.where(d < exact, d, large)


def dsa_mixer(h, w_in, w_out, q_gain, k_gain, idx_k_gain, rel_table):
    B, S, _ = h.shape
    topk = min(TOPK_MAX, S // 4)
    s1 = A_Q_COLS
    s2 = s1 + A_KV_COLS
    s3 = s2 + A_KV_COLS
    s4 = s3 + A_IQ_COLS
    s5 = s4 + IDX_DIM
    proj = h @ w_in
    q, k, v, iq, ik, iw = jnp.split(proj, [s1, s2, s3, s4, s5], axis=-1)
    q = rmsnorm(q.reshape(B, S, A_HEADS, A_HEAD_DIM), q_gain)
    k = rmsnorm(k.reshape(B, S, A_KV_HEADS, A_HEAD_DIM), k_gain)
    v = v.reshape(B, S, A_KV_HEADS, A_HEAD_DIM)
    iq = iq.reshape(B, S, IDX_HEADS, IDX_DIM)
    ik = rmsnorm(ik, idx_k_gain)
    iw = iw * (IDX_HEADS ** -0.5)
    key_pos = jnp.arange(S, dtype=jnp.int32)
    scale = A_HEAD_DIM ** -0.5

    def block(i):
        t0 = i * Q_BLOCK
        qpos = t0 + jnp.arange(Q_BLOCK, dtype=jnp.int32)
        iq_b = lax.dynamic_slice_in_dim(iq, t0, Q_BLOCK, axis=1)
        iw_b = lax.dynamic_slice_in_dim(iw, t0, Q_BLOCK, axis=1)
        q_b = lax.dynamic_slice_in_dim(q, t0, Q_BLOCK, axis=1)
        raw = jnp.einsum('bthd,bsd->bths', iq_b, ik) * (IDX_DIM ** -0.5)
        score = jnp.einsum('bth,bths->bts', iw_b, jax.nn.relu(raw)).astype(jnp.float32)
        causal = key_pos[None, :] <= qpos[:, None]
        score = jnp.where(causal[None], score, -jnp.inf)
        _, sel = lax.top_k(score, topk)
        valid = sel <= qpos[None, :, None]
        kg = jax.vmap(lambda kk, ii: kk[ii])(k, sel)
        vg = jax.vmap(lambda vv, ii: vv[ii])(v, sel)
        qg = q_b.reshape(B, Q_BLOCK, A_KV_HEADS, A_GROUP, A_HEAD_DIM)
        logits = jnp.einsum('btjgd,btnjd->btjgn', qg, kg).astype(jnp.float32) * scale
        bias = rel_table[t5_bucket(qpos[None, :, None] - sel)].astype(jnp.float32)
        bias = bias.reshape(B, Q_BLOCK, topk, A_KV_HEADS, A_GROUP).transpose(0, 1, 3, 4, 2)
        logits = jnp.where(valid[:, :, None, None, :], logits + bias, -jnp.inf)
        p = jax.nn.softmax(logits, axis=-1).astype(vg.dtype)
        o = jnp.einsum('btjgn,btnjd->btjgd', p, vg)
        return o.reshape(B, Q_BLOCK, A_HEADS * A_HEAD_DIM)

    o = lax.map(block, jnp.arange(S // Q_BLOCK))
    o = o.transpose(1, 0, 2, 3).reshape(B, S, A_HEADS * A_HEAD_DIM)
    return o @ w_out


def hgrn2_mixer(h, w_in, w_out, lower_bound, o_gain):
    B, S, _ = h.shape
    nc = S // CHUNK
    proj = h @ w_in
    q, f, i, g = jnp.split(proj, 4, axis=-1)
    f_gate = lower_bound + (1.0 - lower_bound) * jax.nn.sigmoid(f.astype(jnp.float32))
    k_in = 1.0 - f_gate
    log_f = jnp.log(f_gate)

    def to_chunks(a, d):
        return a.astype(jnp.float32).reshape(B, nc, CHUNK, B_HEADS, d).transpose(1, 0, 3, 2, 4)

    qc = to_chunks(q, B_KEY_DIM) * (B_KEY_DIM ** -0.5)
    kc = to_chunks(k_in, B_KEY_DIM)
    vc = to_chunks(i, B_VAL_DIM)
    lfc = to_chunks(log_f, B_KEY_DIM)
    tri = jnp.tril(jnp.ones((CHUNK, CHUNK), dtype=bool))

    def step(state, xs):
        qq, kk, vv, lf = xs
        b = jnp.cumsum(lf, axis=2)
        o_inter = jnp.einsum('bhtd,bhde->bhte', qq * jnp.exp(b), state)
        diff = b[:, :, :, None, :] - b[:, :, None, :, :]
        decay = jnp.exp(jnp.where(tri[:, :, None], diff, -jnp.inf))
        a = jnp.einsum('bhtd,bhsd,bhtsd->bhts', qq, kk, decay)
        o_intra = jnp.einsum('bhts,bhse->bhte', a, vv)
        b_last = b[:, :, -1, :]
        k_dec = kk * jnp.exp(b_last[:, :, None, :] - b)
        state = state * jnp.exp(b_last)[..., None] + jnp.einsum('bhsd,bhse->bhde', k_dec, vv)
        return state, o_inter + o_intra

    s0 = jnp.zeros((B, B_HEADS, B_KEY_DIM, B_VAL_DIM), jnp.float32)
    _, ys = lax.scan(step, s0, (qc, kc, vc, lfc))
    o = ys.transpose(1, 0, 3, 2, 4).reshape(B, S, B_HEADS, B_VAL_DIM)
    o = rmsnorm(o, o_gain.reshape(B_HEADS, B_VAL_DIM)).reshape(B, S, D_MODEL)
    o = (o * jax.nn.silu(g.astype(jnp.float32))).astype(h.dtype)
    return o @ w_out


def setup_inputs(seed: int = 0) -> dict:
    key = jax.random.key(seed)
    ks = jax.random.split(key, 16)
    nrm = jax.random.normal
    f32 = jnp.float32
    return {
        'x': nrm(ks[0], (BATCH, SEQ, D_MODEL), f32),
        'norm_gains': 1.0 + 0.02 * nrm(ks[1], (DEPTH, 3, D_MODEL), f32),
        'ffn_w_gate': nrm(ks[2], (DEPTH, 2, D_MODEL, D_FF), f32) * D_MODEL ** -0.5,
        'ffn_w_up': nrm(ks[3], (DEPTH, 2, D_MODEL, D_FF), f32) * D_MODEL ** -0.5,
        'ffn_w_down': nrm(ks[4], (DEPTH, 2, D_FF, D_MODEL), f32) * D_FF ** -0.5,
        'dsa_w_in': nrm(ks[5], (N_A_LAYERS, D_MODEL, A_IN_COLS), f32) * D_MODEL ** -0.5,
        'dsa_w_out': nrm(ks[6], (N_A_LAYERS, A_Q_COLS, D_MODEL), f32) * A_Q_COLS ** -0.5,
        'dsa_q_gain': 1.0 + 0.02 * nrm(ks[7], (N_A_LAYERS, A_HEAD_DIM), f32),
        'dsa_k_gain': 1.0 + 0.02 * nrm(ks[8], (N_A_LAYERS, A_HEAD_DIM), f32),
        'dsa_idx_k_gain': 1.0 + 0.02 * nrm(ks[9], (N_A_LAYERS, IDX_DIM), f32),
        'rel_bias': 0.5 * nrm(ks[10], (REL_BUCKETS, A_HEADS), f32),
        'hgrn_w_in': nrm(ks[11], (N_B_LAYERS, D_MODEL, B_IN_COLS), f32) * D_MODEL ** -0.5,
        'hgrn_w_out': nrm(ks[12], (N_B_LAYERS, D_MODEL, D_MODEL), f32) * D_MODEL ** -0.5,
        'hgrn_lb_logits': 0.5 * nrm(ks[13], (DEPTH, D_MODEL), f32),
        'hgrn_o_gain': 1.0 + 0.02 * nrm(ks[14], (N_B_LAYERS, D_MODEL), f32),
    }


def reference(x, norm_gains, ffn_w_gate, ffn_w_up, ffn_w_down, dsa_w_in, dsa_w_out, dsa_q_gain, dsa_k_gain, dsa_idx_k_gain, rel_bias, hgrn_w_in, hgrn_w_out, hgrn_lb_logits, hgrn_o_gain):
    p_lb = jax.nn.softmax(hgrn_lb_logits.astype(jnp.float32), axis=0)
    lower_bounds = jnp.cumsum(p_lb, axis=0) - p_lb[0]
    for layer in range(DEPTH):
        j = layer // N_MIXERS
        x = x + 0.5 * swiglu(rmsnorm(x, norm_gains[layer, 0]), ffn_w_gate[layer, 0], ffn_w_up[layer, 0], ffn_w_down[layer, 0])
        hm = rmsnorm(x, norm_gains[layer, 1])
        if layer % N_MIXERS == 0:
            x = x + dsa_mixer(hm, dsa_w_in[j], dsa_w_out[j], dsa_q_gain[j], dsa_k_gain[j], dsa_idx_k_gain[j], rel_bias)
        else:
            x = x + hgrn2_mixer(hm, hgrn_w_in[j], hgrn_w_out[j], lower_bounds[layer], hgrn_o_gain[j])
        x = x + 0.5 * swiglu(rmsnorm(x, norm_gains[layer, 2]), ffn_w_gate[layer, 1], ffn_w_up[layer, 1], ffn_w_down[layer, 1])
    return x
```

```python
import functools
import math

import numpy as np
import jax
import jax.numpy as jnp
from jax import lax
from jax.experimental import pallas as pl
from jax.experimental.pallas import tpu as pltpu

F32 = jnp.float32
BF16 = jnp.bfloat16
I32 = jnp.int32

D_MODEL = 2048
DEPTH = 4
N_MIXERS = 2
A_HEADS = 16
A_HEAD_DIM = 128
A_KV_HEADS = 4
A_GROUP = A_HEADS // A_KV_HEADS
IDX_HEADS = 16
IDX_DIM = 64
TOPK_MAX = 256
REL_BUCKETS = 32
REL_MAX_DIST = 128
B_KEY_DIM = 128
B_HEADS = D_MODEL // B_KEY_DIM
D_FF = 5632
EPS = 1e-6

A_Q_COLS = A_HEADS * A_HEAD_DIM
A_KV_COLS = A_KV_HEADS * A_HEAD_DIM
A_IQ_COLS = IDX_HEADS * IDX_DIM
A_MAIN_COLS = A_Q_COLS + 2 * A_KV_COLS + A_IQ_COLS
A_IN_COLS = A_MAIN_COLS + IDX_DIM + IDX_HEADS

LANES = 128
V7X_VMEM_BYTES = 64 * 1024 * 1024
VMEM_LIMIT_BYTES = 56 * 1024 * 1024

INT_MIN = -(2 ** 31)

FFN_TM = 1024
FFN_TF = 512
NMM_TM = 1024
NMM_TN = 512
OUT_TM = 512
QB = 128
SCORE_CHUNK = 512
COUNT_ROWS = 256
HG_CHUNK = 128
HG_HEADS_PER_STEP = 2
HG_LEVELS = 7


def _params(*semantics):
    return pltpu.CompilerParams(dimension_semantics=semantics, vmem_limit_bytes=VMEM_LIMIT_BYTES)


def _resident(shape, index_map):
    return pl.BlockSpec(shape, index_map, pipeline_mode=pl.Buffered(1))


def _rms_scale(x):
    return lax.rsqrt(jnp.mean(x * x, axis=-1, keepdims=True) + EPS)


def _ffn_body(x_ref, g_ref, wg_ref, wu_ref, wd_ref, o_ref, h_ref):
    j = pl.program_id(1)

    @pl.when(j == 0)
    def _():
        x = x_ref[...]
        h_ref[...] = (x * _rms_scale(x) * g_ref[...]).astype(BF16)
        o_ref[...] = jnp.zeros_like(o_ref)

    h = h_ref[...]
    gate = jnp.dot(h, wg_ref[...], preferred_element_type=F32)
    up = jnp.dot(h, wu_ref[...], preferred_element_type=F32)
    act = (gate * jax.nn.sigmoid(gate) * up).astype(BF16)
    o_ref[...] += jnp.dot(act, wd_ref[...], preferred_element_type=F32)

    @pl.when(j == pl.num_programs(1) - 1)
    def _():
        o_ref[...] = x_ref[...] + 0.5 * o_ref[...]


def _ffn(x, gain, wg, wu, wd):
    s, d = x.shape
    f = wg.shape[1]
    tm = min(FFN_TM, s)
    return pl.pallas_call(
        _ffn_body,
        grid=(s // tm, f // FFN_TF),
        in_specs=[
            pl.BlockSpec((tm, d), lambda i, j: (i, 0)),
            pl.BlockSpec((1, d), lambda i, j: (0, 0)),
            pl.BlockSpec((d, FFN_TF), lambda i, j: (0, j)),
            pl.BlockSpec((d, FFN_TF), lambda i, j: (0, j)),
            pl.BlockSpec((FFN_TF, d), lambda i, j: (j, 0)),
        ],
        out_specs=pl.BlockSpec((tm, d), lambda i, j: (i, 0)),
        out_shape=jax.ShapeDtypeStruct((s, d), F32),
        scratch_shapes=[pltpu.VMEM((tm, d), BF16)],
        compiler_params=_params("parallel", "arbitrary"),
        name="ffn",
    )(x, gain.reshape(1, d), wg, wu, wd)


def _nmm_body(*refs, n_extra, epilogue):
    x_ref, g_ref, w_ref = refs[:3]
    extra = refs[3:3 + n_extra]
    out_refs = refs[3 + n_extra:-1]
    h_ref = refs[-1]
    n = pl.program_id(1)

    @pl.when(n == 0)
    def _():
        x = x_ref[...]
        h_ref[...] = (x * _rms_scale(x) * g_ref[...]).astype(BF16)

    y = jnp.dot(h_ref[...], w_ref[...], preferred_element_type=F32)
    epilogue(n, y, extra, out_refs)


def _norm_matmul(x, gain, w, extra, extra_specs, out_shapes, out_specs, epilogue):
    s, d = x.shape
    ncols = w.shape[1]
    tm = min(NMM_TM, s)
    return pl.pallas_call(
        functools.partial(_nmm_body, n_extra=len(extra), epilogue=epilogue),
        grid=(s // tm, ncols // NMM_TN),
        in_specs=[
            pl.BlockSpec((tm, d), lambda i, n: (i, 0)),
            pl.BlockSpec((1, d), lambda i, n: (0, 0)),
            pl.BlockSpec((d, NMM_TN), lambda i, n: (0, n)),
        ] + list(extra_specs),
        out_specs=out_specs,
        out_shape=out_shapes,
        scratch_shapes=[pltpu.VMEM((tm, d), BF16)],
        compiler_params=_params("parallel", "arbitrary"),
        name="norm_matmul",
    )(x, gain.reshape(1, d), w, *extra)


def _head_norm(y, gain, scale):
    parts = []
    for c in range(NMM_TN // A_HEAD_DIM):
        z = y[:, c * A_HEAD_DIM:(c + 1) * A_HEAD_DIM]
        parts.append(z * _rms_scale(z) * (gain * scale))
    return jnp.concatenate(parts, axis=1)


def _dsa_in_epilogue(n, y, extra, out_refs):
    qg_ref, kg_ref, ikg_ref = extra
    main_ref, iw_ref = out_refs
    q_tiles = A_Q_COLS // NMM_TN
    k_tile = q_tiles
    v_tile = k_tile + 1
    iq_tile0 = v_tile + 1
    idx_tile = A_MAIN_COLS // NMM_TN

    @pl.when(n < q_tiles)
    def _():
        main_ref[...] = _head_norm(y, qg_ref[...], A_HEAD_DIM ** -0.5).astype(BF16)

    @pl.when(n == k_tile)
    def _():
        main_ref[...] = _head_norm(y, kg_ref[...], 1.0).astype(BF16)

    @pl.when(n == v_tile)
    def _():
        main_ref[...] = y.astype(BF16)

    @pl.when(jnp.logical_and(n >= iq_tile0, n < idx_tile))
    def _():
        main_ref[...] = (y * IDX_DIM ** -0.5).astype(BF16)

    @pl.when(n == idx_tile)
    def _():
        z = y[:, :LANES]
        lane = lax.broadcasted_iota(I32, z.shape, 1)
        ms = jnp.sum(jnp.where(lane < IDX_DIM, z * z, 0.0), axis=-1, keepdims=True) * (1.0 / IDX_DIM)
        ikn = z * lax.rsqrt(ms + EPS) * ikg_ref[...]
        main_ref[...] = jnp.concatenate(
            [ikn, jnp.zeros((y.shape[0], NMM_TN - LANES), F32)], axis=1).astype(BF16)
        iw_ref[...] = z * IDX_HEADS ** -0.5


def _dsa_in_proj(x, gain, w_cat, q_gain, k_gain, ik_gain_padded):
    s = x.shape[0]
    tm = min(NMM_TM, s)
    ncols = w_cat.shape[1]
    vec = lambda i, n: (0, 0)
    return _norm_matmul(
        x, gain, w_cat,
        extra=(q_gain.reshape(1, A_HEAD_DIM), k_gain.reshape(1, A_HEAD_DIM), ik_gain_padded),
        extra_specs=[pl.BlockSpec((1, A_HEAD_DIM), vec), pl.BlockSpec((1, A_HEAD_DIM), vec),
                     pl.BlockSpec((1, LANES), vec)],
        out_shapes=(jax.ShapeDtypeStruct((s, ncols), BF16), jax.ShapeDtypeStruct((s, LANES), F32)),
        out_specs=(pl.BlockSpec((tm, NMM_TN), lambda i, n: (i, n)),
                   pl.BlockSpec((tm, LANES), lambda i, n: (i, 0))),
        epilogue=_dsa_in_epilogue,
    )


def _identity_epilogue(n, y, extra, out_refs):
    out_refs[0][...] = y


def _hgrn_in_proj(x, gain, w):
    s = x.shape[0]
    tm = min(NMM_TM, s)
    return _norm_matmul(
        x, gain, w, extra=(), extra_specs=[],
        out_shapes=jax.ShapeDtypeStruct((s, w.shape[1]), F32),
        out_specs=pl.BlockSpec((tm, NMM_TN), lambda i, n: (i, n)),
        epilogue=_identity_epilogue,
    )


def _out_proj_body(a_ref, w_ref, r_ref, o_ref):
    o_ref[...] = r_ref[...] + jnp.dot(a_ref[...], w_ref[...], preferred_element_type=F32)


def _out_proj(a, w, res):
    s, k = a.shape
    d = w.shape[1]
    tm = min(OUT_TM, s)
    return pl.pallas_call(
        _out_proj_body,
        grid=(s // tm,),
        in_specs=[
            pl.BlockSpec((tm, k), lambda i: (i, 0)),
            _resident((k, d), lambda i: (0, 0)),
            pl.BlockSpec((tm, d), lambda i: (i, 0)),
        ],
        out_specs=pl.BlockSpec((tm, d), lambda i: (i, 0)),
        out_shape=jax.ShapeDtypeStruct((s, d), F32),
        compiler_params=_params("parallel"),
        name="out_proj",
    )(a, w, res)


def _t5_bucket_table():
    exact = REL_BUCKETS // 2
    d = np.arange(2 * QB)
    df = np.maximum(d, 1).astype(np.float64)
    val = np.log(df / exact) / math.log(REL_MAX_DIST / exact) * (REL_BUCKETS - exact)
    big = d >= exact
    frac = np.abs(val[big] - np.round(val[big]))
    assert np.all((frac > 1e-4) | (frac == 0.0))
    large = np.minimum(exact + np.floor(val + 1e-9).astype(np.int64), REL_BUCKETS - 1)
    return np.where(d < exact, d, large).astype(np.int32)


def _near_bucket_tiles():
    table = _t5_bucket_table()
    r = np.arange(QB)[:, None]
    t = np.arange(QB)[None, :]
    tiles = [table[np.maximum(n * QB + t - r, 0)] for n in range(2)]
    return np.stack(tiles).astype(np.int32)


def _bias_body(rel_ref, bkt_ref, o_ref):
    j = pl.program_id(1)
    bkt = bkt_ref[0]
    for g in range(A_GROUP):
        h = j * A_GROUP + g
        far = rel_ref[REL_BUCKETS - 1, h]
        acc = jnp.zeros(bkt.shape, F32)
        for b in range(REL_BUCKETS):
            acc = jnp.where(bkt == b, rel_ref[b, h] - far, acc)
        o_ref[0, 0, :, g * QB:(g + 1) * QB] = acc


def _near_bias(rel_bias):
    bkt = jnp.asarray(_near_bucket_tiles())
    return pl.pallas_call(
        _bias_body,
        grid=(2, A_KV_HEADS),
        in_specs=[
            pl.BlockSpec(memory_space=pltpu.SMEM),
            pl.BlockSpec((1, QB, QB), lambda n, j: (n, 0, 0)),
        ],
        out_specs=pl.BlockSpec((1, 1, QB, A_GROUP * QB), lambda n, j: (n, j, 0, 0)),
        out_shape=jax.ShapeDtypeStruct((2, A_KV_HEADS, QB, A_GROUP * QB), F32),
        compiler_params=_params("arbitrary", "arbitrary"),
        name="near_bias",
    )(rel_bias, bkt)


def _attn_body(q_ref, iq_ref, iw_ref, k_ref, ik_ref, vt_ref, bias_ref, o_ref,
               iqt_ref, qt_ref, sc_ref, acc_ref, m_ref, l_ref, *, topk):
    i = pl.program_id(0)
    gq = A_GROUP * QB

    iqt = iq_ref[...].astype(F32).T
    iqt_ref[...] = jnp.zeros_like(iqt_ref)
    for h in range(IDX_HEADS):
        iqt_ref[0:IDX_DIM, h * QB:(h + 1) * QB] = iqt[h * IDX_DIM:(h + 1) * IDX_DIM, :].astype(BF16)
    qt = q_ref[...].astype(F32).T
    for h in range(A_HEADS):
        qt_ref[:, h * QB:(h + 1) * QB] = qt[h * A_HEAD_DIM:(h + 1) * A_HEAD_DIM, :].astype(BF16)
    iwt = iw_ref[...].T

    n_chunks = (i * QB + QB + SCORE_CHUNK - 1) // SCORE_CHUNK

    def score_chunk(c, carry):
        row0 = pl.multiple_of(c * SCORE_CHUNK, SCORE_CHUNK)
        ikc = ik_ref[pl.ds(row0, SCORE_CHUNK), :]
        sc = jnp.zeros((SCORE_CHUNK, QB), F32)
        for hp in range(IDX_HEADS // 2):
            raw = jnp.dot(ikc, iqt_ref[:, hp * 2 * QB:(hp + 1) * 2 * QB], preferred_element_type=F32)
            for u in range(2):
                h = 2 * hp + u
                sc = sc + jnp.maximum(raw[:, u * QB:(u + 1) * QB], 0.0) * iwt[IDX_DIM + h:IDX_DIM + h + 1, :]
        key_pos = row0 + lax.broadcasted_iota(I32, sc.shape, 0)
        q_pos = i * QB + lax.broadcasted_iota(I32, sc.shape, 1)
        bits = pltpu.bitcast(sc, I32)
        skey = jnp.where(bits >= 0, bits, bits ^ 0x7FFFFFFF)
        sc_ref[pl.ds(row0, SCORE_CHUNK), :] = jnp.where(key_pos <= q_pos, skey, INT_MIN)
        return carry

    lax.fori_loop(0, n_chunks, score_chunk, 0)

    n_count = n_chunks * (SCORE_CHUNK // COUNT_ROWS)

    def count_ge(cand):
        def body(r, acc):
            row0 = pl.multiple_of(r * COUNT_ROWS, COUNT_ROWS)
            blk = sc_ref[pl.ds(row0, COUNT_ROWS), :]
            hit = jnp.where(blk >= cand, 1, 0).astype(I32)
            return acc + jnp.sum(hit.reshape(COUNT_ROWS // 8, 8, QB), axis=0)
        acc = lax.fori_loop(0, n_count, body, jnp.zeros((8, QB), I32))
        return jnp.sum(acc, axis=0, keepdims=True)

    def bit_step(it, ans):
        bit = 31 - it
        cand = jnp.where(it == 0, jnp.zeros_like(ans), ans | (jnp.int32(1) << bit))
        return jnp.where(count_ge(cand) >= topk, cand, ans)

    ans = lax.fori_loop(0, 32, bit_step, jnp.full((1, QB), INT_MIN, I32))
    thr = jnp.maximum(ans, INT_MIN + 1)

    for j in range(A_KV_HEADS):
        m_ref[...] = jnp.full_like(m_ref, -1e30)
        l_ref[...] = jnp.zeros_like(l_ref)
        acc_ref[...] = jnp.zeros_like(acc_ref)

        def kv_step(c, near):
            row0 = pl.multiple_of(c * QB, QB)
            kc = k_ref[pl.ds(row0, QB), j * A_HEAD_DIM:(j + 1) * A_HEAD_DIM]
            st = jnp.dot(kc, qt_ref[:, j * gq:(j + 1) * gq], preferred_element_type=F32)
            sel = sc_ref[pl.ds(row0, QB), :] >= thr
            madd = jnp.where(sel, 0.0, -jnp.inf).astype(F32)
            st = st + jnp.concatenate([madd] * A_GROUP, axis=1)
            if near is not None:
                st = st + bias_ref[near, j]
            m_old = m_ref[...]
            m_new = jnp.maximum(m_old, jnp.max(st, axis=0, keepdims=True))
            alpha = jnp.exp(m_old - m_new)
            p = jnp.exp(st - m_new)
            l_ref[...] = alpha * l_ref[...] + jnp.sum(p, axis=0, keepdims=True)
            m_ref[...] = m_new
            vt = vt_ref[c, j * A_HEAD_DIM:(j + 1) * A_HEAD_DIM, :]
            acc_ref[...] = acc_ref[...] * alpha + jnp.dot(vt, p.astype(BF16), preferred_element_type=F32)

        def far_step(c, carry):
            kv_step(c, None)
            return carry

        lax.fori_loop(0, jnp.maximum(i - 1, 0), far_step, 0)

        @pl.when(i >= 1)
        def _():
            kv_step(i - 1, 1)

        kv_step(i, 0)

        out_t = acc_ref[...] * (1.0 / l_ref[...])
        for g in range(A_GROUP):
            h = j * A_GROUP + g
            o_ref[:, h * A_HEAD_DIM:(h + 1) * A_HEAD_DIM] = out_t[:, g * QB:(g + 1) * QB].T.astype(o_ref.dtype)


def _dsa_attention(main, iw, vt3, bias, topk):
    s = main.shape[0]
    nb = s // QB
    q_blk = A_Q_COLS // A_Q_COLS
    del q_blk
    return pl.pallas_call(
        functools.partial(_attn_body, topk=topk),
        grid=(nb,),
        in_specs=[
            pl.BlockSpec((QB, A_Q_COLS), lambda i: (i, 0)),
            pl.BlockSpec((QB, A_IQ_COLS), lambda i: (i, (A_Q_COLS + 2 * A_KV_COLS) // A_IQ_COLS)),
            pl.BlockSpec((QB, LANES), lambda i: (i, 0)),
            _resident((s, A_KV_COLS), lambda i: (0, A_Q_COLS // A_KV_COLS)),
            _resident((s, LANES), lambda i: (0, A_MAIN_COLS // LANES)),
            _resident((nb, A_KV_COLS, QB), lambda i: (0, 0, 0)),
            _resident((2, A_KV_HEADS, QB, A_GROUP * QB), lambda i: (0, 0, 0, 0)),
        ],
        out_specs=pl.BlockSpec((QB, A_Q_COLS), lambda i: (i, 0)),
        out_shape=jax.ShapeDtypeStruct((s, A_Q_COLS), BF16),
        scratch_shapes=[
            pltpu.VMEM((LANES, IDX_HEADS * QB), BF16),
            pltpu.VMEM((A_HEAD_DIM, A_HEADS * QB), BF16),
            pltpu.VMEM((s, QB), I32),
            pltpu.VMEM((A_HEAD_DIM, A_GROUP * QB), F32),
            pltpu.VMEM((1, A_GROUP * QB), F32),
            pltpu.VMEM((1, A_GROUP * QB), F32),
        ],
        compiler_params=_params("arbitrary"),
        name="dsa_attention",
    )(main, main, iw, main, main, vt3, bias)


def _dsa_mixer(x, gain, w_cat, w_out, q_gain, k_gain, ik_gain_padded, bias):
    s = x.shape[0]
    topk = min(TOPK_MAX, s // 4)
    main, iw = _dsa_in_proj(x, gain, w_cat, q_gain, k_gain, ik_gain_padded)
    v = main[:, A_Q_COLS + A_KV_COLS:A_Q_COLS + 2 * A_KV_COLS]
    vt3 = v.reshape(s // QB, QB, A_KV_COLS).transpose(0, 2, 1)
    o = _dsa_attention(main, iw, vt3, bias, topk)
    return _out_proj(o, w_out, x)


def _hgrn_sum_matrices():
    c = HG_CHUNK
    t = np.arange(c)[:, None]
    s = np.arange(c)[None, :]
    mats = []
    for l in range(1, HG_LEVELS + 1):
        blk = 1 << l
        m = (t // blk) * blk + blk // 2
        upper = t >= m
        mats.append(np.where(upper, (s >= m) & (s <= t), (s > t) & (s <= m - 1)))
    mats.append(s <= t)
    mats.append(s > t)
    return np.concatenate(mats, axis=0).astype(np.float32)


def _hgrn_body(q_ref, f_ref, i_ref, g_ref, lbl_ref, og_ref, gm_ref, o_ref, st_ref, *, layer):
    c = pl.program_id(1)
    ch = HG_CHUNK

    @pl.when(c == 0)
    def _():
        st_ref[...] = jnp.zeros_like(st_ref)

    logits = lbl_ref[...]
    e = jnp.exp(logits - jnp.max(logits, axis=0, keepdims=True))
    p = e / jnp.sum(e, axis=0, keepdims=True)
    csum = p[0:1, :]
    for r in range(1, layer + 1):
        csum = csum + p[r:r + 1, :]
    lb = csum - p[0:1, :]

    fgate = lb + (1.0 - lb) * jax.nn.sigmoid(f_ref[...])
    kin = 1.0 - fgate
    logf = jnp.log(fgate)
    logf_hi = logf.astype(BF16)
    logf_lo = (logf - logf_hi.astype(F32)).astype(BF16)
    gm = gm_ref[...]
    dec = (jnp.dot(gm, logf_hi, preferred_element_type=F32)
           + jnp.dot(gm, logf_lo, preferred_element_type=F32))

    row = lax.broadcasted_iota(I32, (ch, ch), 0)
    col = lax.broadcasted_iota(I32, (ch, ch), 1)
    nt = (((1,), (1,)), ((), ()))

    for hh in range(HG_HEADS_PER_STEP):
        sl = slice(hh * B_KEY_DIM, (hh + 1) * B_KEY_DIM)
        qh = q_ref[:, sl] * (B_KEY_DIM ** -0.5)
        kh = kin[:, sl]
        vh = i_ref[:, sl]
        b_incl = dec[HG_LEVELS * ch:(HG_LEVELS + 1) * ch, sl]
        b_rest = dec[(HG_LEVELS + 1) * ch:(HG_LEVELS + 2) * ch, sl]
        state = st_ref[hh]

        o = lax.dot_general((qh * jnp.exp(b_incl)).astype(BF16), state.astype(BF16), nt,
                            preferred_element_type=F32)
        a = jnp.zeros((ch, ch), F32)
        for l in range(1, HG_LEVELS + 1):
            ex = jnp.exp(dec[(l - 1) * ch:l * ch, sl])
            upper = ((row >> (l - 1)) & 1) == 1
            qt = jnp.where(upper, qh * ex, 0.0).astype(BF16)
            kt = jnp.where(upper, 0.0, kh * ex).astype(BF16)
            al = lax.dot_general(qt, kt, nt, preferred_element_type=F32)
            a = a + jnp.where((row >> l) == (col >> l), al, 0.0)
        o = o + jnp.dot(a.astype(BF16), vh.astype(BF16), preferred_element_type=F32)
        o = o + jnp.sum(qh * kh, axis=-1, keepdims=True) * vh

        kdec = (kh * jnp.exp(b_rest)).astype(BF16)
        st_ref[hh] = (state * jnp.exp(b_incl[ch - 1:ch, :])
                      + jnp.dot(vh.T.astype(BF16), kdec, preferred_element_type=F32))

        on = o * _rms_scale(o) * og_ref[:, sl]
        gate = g_ref[:, sl]
        o_ref[:, sl] = (on * (gate * jax.nn.sigmoid(gate))).astype(o_ref.dtype)


def _hgrn_core(proj, lb_logits, o_gain, layer):
    s = proj.shape[0]
    w = HG_HEADS_PER_STEP * B_KEY_DIM
    nhb = D_MODEL // w
    gm = jnp.asarray(_hgrn_sum_matrices(), dtype=BF16)
    col = lambda off: (lambda hb, c: (c, off * nhb + hb))
    return pl.pallas_call(
        functools.partial(_hgrn_body, layer=layer),
        grid=(nhb, s // HG_CHUNK),
        in_specs=[
            pl.BlockSpec((HG_CHUNK, w), col(0)),
            pl.BlockSpec((HG_CHUNK, w), col(1)),
            pl.BlockSpec((HG_CHUNK, w), col(2)),
            pl.BlockSpec((HG_CHUNK, w), col(3)),
            pl.BlockSpec((DEPTH, w), lambda hb, c: (0, hb)),
            pl.BlockSpec((1, w), lambda hb, c: (0, hb)),
            pl.BlockSpec(gm.shape, lambda hb, c: (0, 0)),
        ],
        out_specs=pl.BlockSpec((HG_CHUNK, w), lambda hb, c: (c, hb)),
        out_shape=jax.ShapeDtypeStruct((s, D_MODEL), BF16),
        scratch_shapes=[pltpu.VMEM((HG_HEADS_PER_STEP, B_KEY_DIM, B_KEY_DIM), F32)],
        compiler_params=_params("parallel", "arbitrary"),
        name="hgrn_core",
    )(proj, proj, proj, proj, lb_logits, o_gain.reshape(1, D_MODEL), gm)


def _hgrn_mixer(x, gain, w_in, w_out, lb_logits, o_gain, layer):
    proj = _hgrn_in_proj(x, gain, w_in)
    o = _hgrn_core(proj, lb_logits, o_gain, layer)
    return _out_proj(o, w_out, x)


def kernel(x, norm_gains, ffn_w_gate, ffn_w_up, ffn_w_down, dsa_w_in, dsa_w_out, dsa_q_gain, dsa_k_gain,
           dsa_idx_k_gain, rel_bias, hgrn_w_in, hgrn_w_out, hgrn_lb_logits, hgrn_o_gain):
    assert x.shape[0] == 1 and x.shape[2] == D_MODEL and x.shape[1] % FFN_TM == 0
    h = x[0]
    wg = ffn_w_gate.astype(BF16)
    wu = ffn_w_up.astype(BF16)
    wd = ffn_w_down.astype(BF16)
    pad_cols = NMM_TN - (A_IN_COLS - A_MAIN_COLS)
    dsa_w_cat = jnp.pad(dsa_w_in, ((0, 0), (0, 0), (0, pad_cols))).astype(BF16)
    dsa_wo = dsa_w_out.astype(BF16)
    ik_gain = jnp.pad(dsa_idx_k_gain, ((0, 0), (0, LANES - IDX_DIM)))
    hg_wi = hgrn_w_in.astype(BF16)
    hg_wo = hgrn_w_out.astype(BF16)
    bias = _near_bias(rel_bias)

    for layer in range(DEPTH):
        j = layer // N_MIXERS
        h = _ffn(h, norm_gains[layer, 0], wg[layer, 0], wu[layer, 0], wd[layer, 0])
        if layer % N_MIXERS == 0:
            h = _dsa_mixer(h, norm_gains[layer, 1], dsa_w_cat[j], dsa_wo[j], dsa_q_gain[j], dsa_k_gain[j],
                           ik_gain[j:j + 1], bias)
        else:
            h = _hgrn_mixer(h, norm_gains[layer, 1], hg_wi[j], hg_wo[j], hgrn_lb_logits, hgrn_o_gain[j], layer)
        h = _ffn(h, norm_gains[layer, 2], wg[layer, 1], wu[layer, 1], wd[layer, 1])
    return h[None]
```

```python
import functools
import math

import numpy as np
import jax
import jax.numpy as jnp
from jax import lax
from jax.experimental import pallas as pl
from jax.experimental.pallas import tpu as pltpu

F32 = jnp.float32
BF16 = jnp.bfloat16
I32 = jnp.int32

D_MODEL = 2048
DEPTH = 4
N_MIXERS = 2
A_HEADS = 16
A_HEAD_DIM = 128
A_KV_HEADS = 4
A_GROUP = A_HEADS // A_KV_HEADS
IDX_HEADS = 16
IDX_DIM = 64
TOPK_MAX = 256
REL_BUCKETS = 32
REL_MAX_DIST = 128
B_KEY_DIM = 128
B_HEADS = D_MODEL // B_KEY_DIM
D_FF = 5632
EPS = 1e-6

A_Q_COLS = A_HEADS * A_HEAD_DIM
A_KV_COLS = A_KV_HEADS * A_HEAD_DIM
A_IQ_COLS = IDX_HEADS * IDX_DIM
A_MAIN_COLS = A_Q_COLS + 2 * A_KV_COLS + A_IQ_COLS
A_IN_COLS = A_MAIN_COLS + IDX_DIM + IDX_HEADS

LANES = 128
V7X_VMEM_BYTES = 64 * 1024 * 1024
VMEM_LIMIT_BYTES = 56 * 1024 * 1024

INT_MIN = -(2 ** 31)

FFN_TM = 1024
FFN_TF = 512
NMM_TM = 1024
NMM_TN = 512
OUT_TM = 512
QB = 128
SCORE_CHUNK = 512
COUNT_ROWS = 512
HG_CHUNK = 128
HG_HEADS_PER_STEP = 4
HG_LEVELS = 7


def _params(*semantics):
    return pltpu.CompilerParams(dimension_semantics=semantics, vmem_limit_bytes=VMEM_LIMIT_BYTES)


def _resident(shape, index_map):
    return pl.BlockSpec(shape, index_map, pipeline_mode=pl.Buffered(1))


def _rms_scale(x):
    return lax.rsqrt(jnp.mean(x * x, axis=-1, keepdims=True) + EPS)


def _ffn_body(x_ref, g_ref, wg_ref, wu_ref, wd_ref, o_ref, h_ref):
    j = pl.program_id(1)

    @pl.when(j == 0)
    def _():
        x = x_ref[...]
        h_ref[...] = (x * _rms_scale(x) * g_ref[...]).astype(BF16)
        o_ref[...] = jnp.zeros_like(o_ref)

    h = h_ref[...]
    gate = jnp.dot(h, wg_ref[...], preferred_element_type=F32)
    up = jnp.dot(h, wu_ref[...], preferred_element_type=F32)
    act = (gate * jax.nn.sigmoid(gate) * up).astype(BF16)
    o_ref[...] += jnp.dot(act, wd_ref[...], preferred_element_type=F32)

    @pl.when(j == pl.num_programs(1) - 1)
    def _():
        o_ref[...] = x_ref[...] + 0.5 * o_ref[...]


def _ffn(x, gain, wg, wu, wd, lead=()):
    s, d = x.shape
    f = wg.shape[-1]
    tm = min(FFN_TM, s)
    squeeze = (None,) * len(lead)
    return pl.pallas_call(
        _ffn_body,
        grid=(s // tm, f // FFN_TF),
        in_specs=[
            pl.BlockSpec((tm, d), lambda i, j: (i, 0)),
            pl.BlockSpec((1, d), lambda i, j: (0, 0)),
            pl.BlockSpec(squeeze + (d, FFN_TF), lambda i, j: lead + (0, j)),
            pl.BlockSpec(squeeze + (d, FFN_TF), lambda i, j: lead + (0, j)),
            pl.BlockSpec(squeeze + (FFN_TF, d), lambda i, j: lead + (j, 0)),
        ],
        out_specs=pl.BlockSpec((tm, d), lambda i, j: (i, 0)),
        out_shape=jax.ShapeDtypeStruct((s, d), F32),
        scratch_shapes=[pltpu.VMEM((tm, d), BF16)],
        compiler_params=_params("parallel", "arbitrary"),
        name="ffn",
    )(x, gain.reshape(1, d), wg, wu, wd)


def _nmm_body(*refs, n_extra, epilogue):
    x_ref, g_ref, w_ref = refs[:3]
    extra = refs[3:3 + n_extra]
    out_refs = refs[3 + n_extra:-1]
    h_ref = refs[-1]
    n = pl.program_id(1)

    @pl.when(n == 0)
    def _():
        x = x_ref[...]
        h_ref[...] = (x * _rms_scale(x) * g_ref[...]).astype(BF16)

    y = jnp.dot(h_ref[...], w_ref[...], preferred_element_type=F32)
    epilogue(n, y, extra, out_refs)


def _norm_matmul(x, gain, w, layer, extra, extra_specs, out_shapes, out_specs, epilogue):
    s, d = x.shape
    ncols = w.shape[-1]
    tm = min(NMM_TM, s)
    return pl.pallas_call(
        functools.partial(_nmm_body, n_extra=len(extra), epilogue=epilogue),
        grid=(s // tm, ncols // NMM_TN),
        in_specs=[
            pl.BlockSpec((tm, d), lambda i, n: (i, 0)),
            pl.BlockSpec((1, d), lambda i, n: (0, 0)),
            pl.BlockSpec((None, d, NMM_TN), lambda i, n: (layer, 0, n)),
        ] + list(extra_specs),
        out_specs=out_specs,
        out_shape=out_shapes,
        scratch_shapes=[pltpu.VMEM((tm, d), BF16)],
        compiler_params=_params("parallel", "arbitrary"),
        name="norm_matmul",
    )(x, gain.reshape(1, d), w, *extra)


def _head_norm(y, gain, scale):
    parts = []
    for c in range(NMM_TN // A_HEAD_DIM):
        z = y[:, c * A_HEAD_DIM:(c + 1) * A_HEAD_DIM]
        parts.append(z * _rms_scale(z) * (gain * scale))
    return jnp.concatenate(parts, axis=1)


def _dsa_in_epilogue(n, y, extra, out_refs):
    qg_ref, kg_ref, ikg_ref = extra
    main_ref, iw_ref = out_refs
    q_tiles = A_Q_COLS // NMM_TN
    k_tile = q_tiles
    v_tile = k_tile + 1
    iq_tile0 = v_tile + 1
    idx_tile = A_MAIN_COLS // NMM_TN

    @pl.when(n < q_tiles)
    def _():
        main_ref[...] = _head_norm(y, qg_ref[...], A_HEAD_DIM ** -0.5).astype(BF16)

    @pl.when(n == k_tile)
    def _():
        main_ref[...] = _head_norm(y, kg_ref[...], 1.0).astype(BF16)

    @pl.when(n == v_tile)
    def _():
        main_ref[...] = y.astype(BF16)

    @pl.when(jnp.logical_and(n >= iq_tile0, n < idx_tile))
    def _():
        main_ref[...] = (y * IDX_DIM ** -0.5).astype(BF16)

    @pl.when(n == idx_tile)
    def _():
        z = y[:, :LANES]
        lane = lax.broadcasted_iota(I32, z.shape, 1)
        ms = jnp.sum(jnp.where(lane < IDX_DIM, z * z, 0.0), axis=-1, keepdims=True) * (1.0 / IDX_DIM)
        ikn = z * lax.rsqrt(ms + EPS) * ikg_ref[...]
        main_ref[...] = jnp.concatenate(
            [ikn, jnp.zeros((y.shape[0], NMM_TN - LANES), F32)], axis=1).astype(BF16)
        iw_ref[...] = z * IDX_HEADS ** -0.5


def _dsa_in_proj(x, gain, w_cat, layer, q_gain, k_gain, ik_gain_padded):
    s = x.shape[0]
    tm = min(NMM_TM, s)
    ncols = w_cat.shape[-1]
    vec = lambda i, n: (0, 0)
    return _norm_matmul(
        x, gain, w_cat, layer,
        extra=(q_gain.reshape(1, A_HEAD_DIM), k_gain.reshape(1, A_HEAD_DIM), ik_gain_padded),
        extra_specs=[pl.BlockSpec((1, A_HEAD_DIM), vec), pl.BlockSpec((1, A_HEAD_DIM), vec),
                     pl.BlockSpec((1, LANES), vec)],
        out_shapes=(jax.ShapeDtypeStruct((s, ncols), BF16), jax.ShapeDtypeStruct((s, LANES), F32)),
        out_specs=(pl.BlockSpec((tm, NMM_TN), lambda i, n: (i, n)),
                   pl.BlockSpec((tm, LANES), lambda i, n: (i, 0))),
        epilogue=_dsa_in_epilogue,
    )


def _identity_epilogue(n, y, extra, out_refs):
    out_refs[0][...] = y


def _hgrn_in_proj(x, gain, w, layer):
    s = x.shape[0]
    tm = min(NMM_TM, s)
    return _norm_matmul(
        x, gain, w, layer, extra=(), extra_specs=[],
        out_shapes=jax.ShapeDtypeStruct((s, w.shape[-1]), F32),
        out_specs=pl.BlockSpec((tm, NMM_TN), lambda i, n: (i, n)),
        epilogue=_identity_epilogue,
    )


def _out_proj_body(a_ref, w_ref, r_ref, o_ref):
    o_ref[...] = r_ref[...] + jnp.dot(a_ref[...], w_ref[...], preferred_element_type=F32)


def _out_proj(a, w, layer, res):
    s, k = a.shape
    d = w.shape[-1]
    tm = min(OUT_TM, s)
    return pl.pallas_call(
        _out_proj_body,
        grid=(s // tm,),
        in_specs=[
            pl.BlockSpec((tm, k), lambda i: (i, 0)),
            _resident((None, k, d), lambda i: (layer, 0, 0)),
            pl.BlockSpec((tm, d), lambda i: (i, 0)),
        ],
        out_specs=pl.BlockSpec((tm, d), lambda i: (i, 0)),
        out_shape=jax.ShapeDtypeStruct((s, d), F32),
        compiler_params=_params("parallel"),
        name="out_proj",
    )(a, w, res)


def _t5_bucket_table():
    exact = REL_BUCKETS // 2
    d = np.arange(2 * QB)
    df = np.maximum(d, 1).astype(np.float64)
    val = np.log(df / exact) / math.log(REL_MAX_DIST / exact) * (REL_BUCKETS - exact)
    big = d >= exact
    frac = np.abs(val[big] - np.round(val[big]))
    assert np.all((frac > 1e-4) | (frac == 0.0))
    large = np.minimum(exact + np.floor(val + 1e-9).astype(np.int64), REL_BUCKETS - 1)
    return np.where(d < exact, d, large).astype(np.int32)


def _near_bucket_tiles():
    table = _t5_bucket_table()
    r = np.arange(QB)[:, None]
    t = np.arange(QB)[None, :]
    tiles = [table[np.maximum(n * QB + t - r, 0)] for n in range(2)]
    return np.stack(tiles).astype(np.int32)


def _bias_body(rel_ref, bkt_ref, o_ref):
    j = pl.program_id(1)
    bkt = bkt_ref[0]
    for g in range(A_GROUP):
        h = j * A_GROUP + g
        far = rel_ref[REL_BUCKETS - 1, h]
        acc = jnp.zeros(bkt.shape, F32)
        for b in range(REL_BUCKETS):
            acc = jnp.where(bkt == b, rel_ref[b, h] - far, acc)
        o_ref[0, 0, :, g * QB:(g + 1) * QB] = acc


def _near_bias(rel_bias):
    bkt = jnp.asarray(_near_bucket_tiles())
    return pl.pallas_call(
        _bias_body,
        grid=(2, A_KV_HEADS),
        in_specs=[
            pl.BlockSpec(memory_space=pltpu.SMEM),
            pl.BlockSpec((1, QB, QB), lambda n, j: (n, 0, 0)),
        ],
        out_specs=pl.BlockSpec((1, 1, QB, A_GROUP * QB), lambda n, j: (n, j, 0, 0)),
        out_shape=jax.ShapeDtypeStruct((2, A_KV_HEADS, QB, A_GROUP * QB), F32),
        compiler_params=_params("arbitrary", "arbitrary"),
        name="near_bias",
    )(rel_bias, bkt)


def _attn_body(q_ref, iq_ref, iw_ref, k_ref, ik_ref, vt_ref, bias_ref, o_ref,
               iqt_ref, qt_ref, sc_ref, acc_ref, m_ref, l_ref, *, topk):
    i = pl.program_id(0)
    gq = A_GROUP * QB

    iqt = iq_ref[...].astype(F32).T
    iqt_ref[...] = jnp.zeros_like(iqt_ref)
    for h in range(IDX_HEADS):
        iqt_ref[0:IDX_DIM, h * QB:(h + 1) * QB] = iqt[h * IDX_DIM:(h + 1) * IDX_DIM, :].astype(BF16)
    qt = q_ref[...].astype(F32).T
    for h in range(A_HEADS):
        qt_ref[:, h * QB:(h + 1) * QB] = qt[h * A_HEAD_DIM:(h + 1) * A_HEAD_DIM, :].astype(BF16)
    iwt = iw_ref[...].T

    n_chunks = (i * QB + QB + SCORE_CHUNK - 1) // SCORE_CHUNK

    def score_chunk(c, carry):
        row0 = pl.multiple_of(c * SCORE_CHUNK, SCORE_CHUNK)
        ikc = ik_ref[pl.ds(row0, SCORE_CHUNK), :]
        sc = jnp.zeros((SCORE_CHUNK, QB), F32)
        for hp in range(IDX_HEADS // 2):
            raw = jnp.dot(ikc, iqt_ref[:, hp * 2 * QB:(hp + 1) * 2 * QB], preferred_element_type=F32)
            for u in range(2):
                h = 2 * hp + u
                sc = sc + jnp.maximum(raw[:, u * QB:(u + 1) * QB], 0.0) * iwt[IDX_DIM + h:IDX_DIM + h + 1, :]
        key_pos = row0 + lax.broadcasted_iota(I32, sc.shape, 0)
        q_pos = i * QB + lax.broadcasted_iota(I32, sc.shape, 1)
        bits = pltpu.bitcast(sc, I32)
        skey = jnp.where(bits >= 0, bits, bits ^ 0x7FFFFFFF)
        sc_ref[pl.ds(row0, SCORE_CHUNK), :] = jnp.where(key_pos <= q_pos, skey, INT_MIN)
        return carry

    lax.fori_loop(0, n_chunks, score_chunk, 0)

    n_count = n_chunks * (SCORE_CHUNK // COUNT_ROWS)

    def count_ge(cand):
        def body(r, acc):
            row0 = pl.multiple_of(r * COUNT_ROWS, COUNT_ROWS)
            blk = sc_ref[pl.ds(row0, COUNT_ROWS), :]
            hit = jnp.where(blk >= cand, 1, 0).astype(I32)
            return acc + jnp.sum(hit.reshape(COUNT_ROWS // 8, 8, QB), axis=0)
        acc = lax.fori_loop(0, n_count, body, jnp.zeros((8, QB), I32))
        return jnp.sum(acc, axis=0, keepdims=True)

    def bit_step(it, ans):
        bit = 31 - it
        cand = jnp.where(it == 0, jnp.zeros_like(ans), ans | (jnp.int32(1) << bit))
        return jnp.where(count_ge(cand) >= topk, cand, ans)

    ans = lax.fori_loop(0, 32, bit_step, jnp.full((1, QB), INT_MIN, I32))
    thr = jnp.maximum(ans, INT_MIN + 1)

    m_ref[...] = jnp.full_like(m_ref, -1e30)
    l_ref[...] = jnp.zeros_like(l_ref)
    acc_ref[...] = jnp.zeros_like(acc_ref)

    def kv_step(c, near):
        row0 = pl.multiple_of(c * QB, QB)
        sel = sc_ref[pl.ds(row0, QB), :] >= thr
        madd = jnp.where(sel, 0.0, -jnp.inf).astype(F32)
        madd = jnp.concatenate([madd] * A_GROUP, axis=1)
        heads = range(A_KV_HEADS)
        kblk = k_ref[pl.ds(row0, QB), :]
        sts = [jnp.dot(kblk[:, j * A_HEAD_DIM:(j + 1) * A_HEAD_DIM], qt_ref[:, j * gq:(j + 1) * gq],
                       preferred_element_type=F32) for j in heads]
        ps, alphas = [], []
        for j in heads:
            st = sts[j] + madd
            if near is not None:
                st = st + bias_ref[near, j]
            m_old = m_ref[j]
            m_new = jnp.maximum(m_old, jnp.max(st, axis=0, keepdims=True))
            alpha = jnp.exp(m_old - m_new)
            p = jnp.exp(st - m_new)
            l_ref[j] = alpha * l_ref[j] + jnp.sum(p, axis=0, keepdims=True)
            m_ref[j] = m_new
            ps.append(p.astype(BF16))
            alphas.append(alpha)
        vblk = vt_ref[c]
        pvs = [jnp.dot(vblk[j * A_HEAD_DIM:(j + 1) * A_HEAD_DIM, :], ps[j], preferred_element_type=F32)
               for j in heads]
        for j in heads:
            acc_ref[j] = acc_ref[j] * alphas[j] + pvs[j]

    def far_step(c, carry):
        kv_step(c, None)
        return carry

    lax.fori_loop(0, jnp.maximum(i - 1, 0), far_step, 0)

    @pl.when(i >= 1)
    def _():
        kv_step(i - 1, 1)

    kv_step(i, 0)

    for j in range(A_KV_HEADS):
        out_t = acc_ref[j] * (1.0 / l_ref[j])
        for g in range(A_GROUP):
            h = j * A_GROUP + g
            o_ref[:, h * A_HEAD_DIM:(h + 1) * A_HEAD_DIM] = out_t[:, g * QB:(g + 1) * QB].T.astype(o_ref.dtype)


def _dsa_attention(main, iw, vt3, bias, topk):
    s = main.shape[0]
    nb = s // QB
    q_blk = A_Q_COLS // A_Q_COLS
    del q_blk
    return pl.pallas_call(
        functools.partial(_attn_body, topk=topk),
        grid=(nb,),
        in_specs=[
            pl.BlockSpec((QB, A_Q_COLS), lambda i: (i, 0)),
            pl.BlockSpec((QB, A_IQ_COLS), lambda i: (i, (A_Q_COLS + 2 * A_KV_COLS) // A_IQ_COLS)),
            pl.BlockSpec((QB, LANES), lambda i: (i, 0)),
            _resident((s, A_KV_COLS), lambda i: (0, A_Q_COLS // A_KV_COLS)),
            _resident((s, LANES), lambda i: (0, A_MAIN_COLS // LANES)),
            _resident((nb, A_KV_COLS, QB), lambda i: (0, 0, 0)),
            _resident((2, A_KV_HEADS, QB, A_GROUP * QB), lambda i: (0, 0, 0, 0)),
        ],
        out_specs=pl.BlockSpec((QB, A_Q_COLS), lambda i: (i, 0)),
        out_shape=jax.ShapeDtypeStruct((s, A_Q_COLS), BF16),
        scratch_shapes=[
            pltpu.VMEM((LANES, IDX_HEADS * QB), BF16),
            pltpu.VMEM((A_HEAD_DIM, A_HEADS * QB), BF16),
            pltpu.VMEM((s, QB), I32),
            pltpu.VMEM((A_KV_HEADS, A_HEAD_DIM, A_GROUP * QB), F32),
            pltpu.VMEM((A_KV_HEADS, 1, A_GROUP * QB), F32),
            pltpu.VMEM((A_KV_HEADS, 1, A_GROUP * QB), F32),
        ],
        compiler_params=_params("arbitrary"),
        name="dsa_attention",
    )(main, main, iw, main, main, vt3, bias)


def _dsa_mixer(x, gain, w_cat, w_out, layer, q_gain, k_gain, ik_gain_padded, bias):
    s = x.shape[0]
    topk = min(TOPK_MAX, s // 4)
    main, iw = _dsa_in_proj(x, gain, w_cat, layer, q_gain, k_gain, ik_gain_padded)
    v = main[:, A_Q_COLS + A_KV_COLS:A_Q_COLS + 2 * A_KV_COLS]
    vt3 = v.reshape(s // QB, QB, A_KV_COLS).transpose(0, 2, 1)
    o = _dsa_attention(main, iw, vt3, bias, topk)
    return _out_proj(o, w_out, layer, x)


def _hgrn_sum_matrices():
    c = HG_CHUNK
    t = np.arange(c)[:, None]
    s = np.arange(c)[None, :]
    mats = []
    for l in range(1, HG_LEVELS + 1):
        blk = 1 << l
        m = (t // blk) * blk + blk // 2
        upper = t >= m
        mats.append(np.where(upper, (s >= m) & (s <= t), (s > t) & (s <= m - 1)))
    mats.append(s <= t)
    mats.append(s > t)
    return np.concatenate(mats, axis=0).astype(np.float32)


def _hgrn_split_levels():
    t = np.arange(HG_CHUNK)[:, None]
    s = np.arange(HG_CHUNK)[None, :]
    lvl = np.zeros((HG_CHUNK, HG_CHUNK), np.int32)
    for l in range(1, HG_LEVELS + 1):
        lvl[(s < t) & ((t >> l) == (s >> l)) & (((t >> (l - 1)) & 1) == 1) & (((s >> (l - 1)) & 1) == 0)] = l
    assert np.all((lvl > 0) == (s < t))
    return lvl


def _hgrn_body(q_ref, f_ref, i_ref, g_ref, lbl_ref, og_ref, gm_ref, split_ref, o_ref, st_ref, *, layer):
    c = pl.program_id(1)
    ch = HG_CHUNK

    @pl.when(c == 0)
    def _():
        st_ref[...] = jnp.zeros_like(st_ref)

    logits = lbl_ref[...]
    e = jnp.exp(logits - jnp.max(logits, axis=0, keepdims=True))
    p = e / jnp.sum(e, axis=0, keepdims=True)
    csum = p[0:1, :]
    for r in range(1, layer + 1):
        csum = csum + p[r:r + 1, :]
    lb = csum - p[0:1, :]

    fgate = lb + (1.0 - lb) * jax.nn.sigmoid(f_ref[...])
    kin = 1.0 - fgate
    logf = jnp.log(fgate)
    logf_hi = logf.astype(BF16)
    logf_lo = (logf - logf_hi.astype(F32)).astype(BF16)
    logf_split = jnp.concatenate([logf_hi, logf_lo], axis=0)
    decs = [jnp.dot(gm_ref[r * ch:(r + 1) * ch, :], logf_split, preferred_element_type=F32)
            for r in range(HG_LEVELS + 2)]

    row = lax.broadcasted_iota(I32, (ch, B_KEY_DIM), 0)
    split = split_ref[...]
    nt = (((1,), (1,)), ((), ()))
    heads = range(HG_HEADS_PER_STEP)
    levels = range(1, HG_LEVELS + 1)
    sls = [slice(hh * B_KEY_DIM, (hh + 1) * B_KEY_DIM) for hh in heads]

    qs = [q_ref[:, sl] * (B_KEY_DIM ** -0.5) for sl in sls]
    ks = [kin[:, sl] for sl in sls]
    vs = [i_ref[:, sl] for sl in sls]
    b_incl = [decs[HG_LEVELS][:, sl] for sl in sls]
    b_rest = [decs[HG_LEVELS + 1][:, sl] for sl in sls]
    states = [st_ref[hh] for hh in heads]

    zs = []
    for hh in heads:
        for l in levels:
            upper = (row & (1 << (l - 1))) != 0
            ex = jnp.exp(decs[l - 1][:, sls[hh]])
            zs.append((jnp.where(upper, qs[hh], ks[hh]) * ex).astype(BF16))
    prods = [lax.dot_general(z, z, nt, preferred_element_type=F32) for z in zs]
    o_inter = [lax.dot_general((qs[hh] * jnp.exp(b_incl[hh])).astype(BF16), states[hh].astype(BF16), nt,
                               preferred_element_type=F32) for hh in heads]
    new_kv = [jnp.dot(vs[hh].T.astype(BF16), (ks[hh] * jnp.exp(b_rest[hh])).astype(BF16),
                      preferred_element_type=F32) for hh in heads]
    a_mats = []
    for hh in heads:
        a = jnp.zeros((ch, ch), F32)
        for l in levels:
            a = jnp.where(split == l, prods[hh * HG_LEVELS + l - 1], a)
        a_mats.append(a.astype(BF16))
    o_intra = [jnp.dot(a_mats[hh], vs[hh].astype(BF16), preferred_element_type=F32) for hh in heads]

    for hh in heads:
        sl = sls[hh]
        st_ref[hh] = states[hh] * jnp.exp(b_incl[hh][ch - 1:ch, :]) + new_kv[hh]
        o = o_inter[hh] + o_intra[hh] + jnp.sum(qs[hh] * ks[hh], axis=-1, keepdims=True) * vs[hh]
        on = o * _rms_scale(o) * og_ref[:, sl]
        gate = g_ref[:, sl]
        o_ref[:, sl] = (on * (gate * jax.nn.sigmoid(gate))).astype(o_ref.dtype)


def _hgrn_core(proj, lb_logits, o_gain, layer):
    s = proj.shape[0]
    w = HG_HEADS_PER_STEP * B_KEY_DIM
    nhb = D_MODEL // w
    gm = jnp.asarray(np.tile(_hgrn_sum_matrices(), (1, 2)), dtype=BF16)
    split = jnp.asarray(_hgrn_split_levels())
    col = lambda off: (lambda hb, c: (c, off * nhb + hb))
    return pl.pallas_call(
        functools.partial(_hgrn_body, layer=layer),
        grid=(nhb, s // HG_CHUNK),
        in_specs=[
            pl.BlockSpec((HG_CHUNK, w), col(0)),
            pl.BlockSpec((HG_CHUNK, w), col(1)),
            pl.BlockSpec((HG_CHUNK, w), col(2)),
            pl.BlockSpec((HG_CHUNK, w), col(3)),
            pl.BlockSpec((DEPTH, w), lambda hb, c: (0, hb)),
            pl.BlockSpec((1, w), lambda hb, c: (0, hb)),
            pl.BlockSpec(gm.shape, lambda hb, c: (0, 0)),
            pl.BlockSpec(split.shape, lambda hb, c: (0, 0)),
        ],
        out_specs=pl.BlockSpec((HG_CHUNK, w), lambda hb, c: (c, hb)),
        out_shape=jax.ShapeDtypeStruct((s, D_MODEL), BF16),
        scratch_shapes=[pltpu.VMEM((HG_HEADS_PER_STEP, B_KEY_DIM, B_KEY_DIM), F32)],
        compiler_params=_params("parallel", "arbitrary"),
        name="hgrn_core",
    )(proj, proj, proj, proj, lb_logits, o_gain.reshape(1, D_MODEL), gm, split)


def _hgrn_mixer(x, gain, w_in, w_out, mixer_idx, lb_logits, o_gain, layer):
    proj = _hgrn_in_proj(x, gain, w_in, mixer_idx)
    o = _hgrn_core(proj, lb_logits, o_gain, layer)
    return _out_proj(o, w_out, mixer_idx, x)


def kernel(x, norm_gains, ffn_w_gate, ffn_w_up, ffn_w_down, dsa_w_in, dsa_w_out, dsa_q_gain, dsa_k_gain,
           dsa_idx_k_gain, rel_bias, hgrn_w_in, hgrn_w_out, hgrn_lb_logits, hgrn_o_gain):
    assert x.shape[0] == 1 and x.shape[2] == D_MODEL and x.shape[1] % FFN_TM == 0
    h = x.reshape(x.shape[1], D_MODEL)
    wg = ffn_w_gate.astype(BF16)
    wu = ffn_w_up.astype(BF16)
    wd = ffn_w_down.astype(BF16)
    pad_cols = NMM_TN - (A_IN_COLS - A_MAIN_COLS)
    dsa_w_cat = jnp.pad(dsa_w_in, ((0, 0), (0, 0), (0, pad_cols))).astype(BF16)
    dsa_wo = dsa_w_out.astype(BF16)
    ik_gain = jnp.pad(dsa_idx_k_gain, ((0, 0), (0, LANES - IDX_DIM)))
    hg_wi = hgrn_w_in.astype(BF16)
    hg_wo = hgrn_w_out.astype(BF16)
    bias = _near_bias(rel_bias)

    for layer in range(DEPTH):
        j = layer // N_MIXERS
        h = _ffn(h, norm_gains[layer, 0], wg, wu, wd, lead=(layer, 0))
        if layer % N_MIXERS == 0:
            h = _dsa_mixer(h, norm_gains[layer, 1], dsa_w_cat, dsa_wo, j, dsa_q_gain[j], dsa_k_gain[j],
                           ik_gain[j:j + 1], bias)
        else:
            h = _hgrn_mixer(h, norm_gains[layer, 1], hg_wi, hg_wo, j, hgrn_lb_logits, hgrn_o_gain[j], layer)
        h = _ffn(h, norm_gains[layer, 2], wg, wu, wd, lead=(layer, 1))
    return h.reshape(x.shape)
```

```python
import functools
import math

import numpy as np
import jax
import jax.numpy as jnp
from jax import lax
from jax.experimental import pallas as pl
from jax.experimental.pallas import tpu as pltpu

F32 = jnp.float32
BF16 = jnp.bfloat16
I32 = jnp.int32

D_MODEL = 2048
DEPTH = 4
N_MIXERS = 2
A_HEADS = 16
A_HEAD_DIM = 128
A_KV_HEADS = 4
A_GROUP = A_HEADS // A_KV_HEADS
IDX_HEADS = 16
IDX_DIM = 64
TOPK_MAX = 256
REL_BUCKETS = 32
REL_MAX_DIST = 128
B_KEY_DIM = 128
B_HEADS = D_MODEL // B_KEY_DIM
D_FF = 5632
EPS = 1e-6

A_Q_COLS = A_HEADS * A_HEAD_DIM
A_KV_COLS = A_KV_HEADS * A_HEAD_DIM
A_IQ_COLS = IDX_HEADS * IDX_DIM
A_MAIN_COLS = A_Q_COLS + 2 * A_KV_COLS + A_IQ_COLS
A_IN_COLS = A_MAIN_COLS + IDX_DIM + IDX_HEADS

LANES = 128
V7X_VMEM_BYTES = 64 * 1024 * 1024
VMEM_LIMIT_BYTES = 56 * 1024 * 1024

INT_MIN = -(2 ** 31)
MASKED_LOGIT = -1e30
M_INIT = -1e20
VT_ROWS = A_HEAD_DIM + 16

FFN_TM = 1024
FFN_TF = 512
NMM_TM = 1024
NMM_TN = 512
OUT_TM = 512
QB = 128
KB = 2 * QB
SCORE_CHUNK = 512
COUNT_ROWS = 512
HG_CHUNK = 128
HG_HEADS_PER_STEP = 4
HG_LEVELS = 7


def _params(*semantics):
    return pltpu.CompilerParams(dimension_semantics=semantics, vmem_limit_bytes=VMEM_LIMIT_BYTES)


def _resident(shape, index_map):
    return pl.BlockSpec(shape, index_map, pipeline_mode=pl.Buffered(1))


def _rms_scale(x):
    return lax.rsqrt(jnp.mean(x * x, axis=-1, keepdims=True) + EPS)


def _ffn_body(x_ref, g_ref, wg_ref, wu_ref, wd_ref, o_ref, h_ref):
    j = pl.program_id(1)

    @pl.when(j == 0)
    def _():
        x = x_ref[...]
        h_ref[...] = (x * _rms_scale(x) * g_ref[...]).astype(BF16)
        o_ref[...] = jnp.zeros_like(o_ref)

    h = h_ref[...]
    gate = jnp.dot(h, wg_ref[...], preferred_element_type=F32)
    up = jnp.dot(h, wu_ref[...], preferred_element_type=F32)
    act = (gate * jax.nn.sigmoid(gate) * up).astype(BF16)
    o_ref[...] += jnp.dot(act, wd_ref[...], preferred_element_type=F32)

    @pl.when(j == pl.num_programs(1) - 1)
    def _():
        o_ref[...] = x_ref[...] + 0.5 * o_ref[...]


def _ffn(x, gain, wg, wu, wd, lead=()):
    s, d = x.shape
    f = wg.shape[-1]
    tm = min(FFN_TM, s)
    squeeze = (None,) * len(lead)
    return pl.pallas_call(
        _ffn_body,
        grid=(s // tm, f // FFN_TF),
        in_specs=[
            pl.BlockSpec((tm, d), lambda i, j: (i, 0)),
            pl.BlockSpec((1, d), lambda i, j: (0, 0)),
            pl.BlockSpec(squeeze + (d, FFN_TF), lambda i, j: lead + (0, j)),
            pl.BlockSpec(squeeze + (d, FFN_TF), lambda i, j: lead + (0, j)),
            pl.BlockSpec(squeeze + (FFN_TF, d), lambda i, j: lead + (j, 0)),
        ],
        out_specs=pl.BlockSpec((tm, d), lambda i, j: (i, 0)),
        out_shape=jax.ShapeDtypeStruct((s, d), F32),
        scratch_shapes=[pltpu.VMEM((tm, d), BF16)],
        compiler_params=_params("parallel", "arbitrary"),
        name="ffn",
    )(x, gain.reshape(1, d), wg, wu, wd)


def _nmm_body(*refs, n_extra, epilogue):
    x_ref, g_ref, w_ref = refs[:3]
    extra = refs[3:3 + n_extra]
    out_refs = refs[3 + n_extra:-1]
    h_ref = refs[-1]
    n = pl.program_id(1)

    @pl.when(n == 0)
    def _():
        x = x_ref[...]
        h_ref[...] = (x * _rms_scale(x) * g_ref[...]).astype(BF16)

    y = jnp.dot(h_ref[...], w_ref[...], preferred_element_type=F32)
    epilogue(n, y, extra, out_refs)


def _norm_matmul(x, gain, w, layer, extra, extra_specs, out_shapes, out_specs, epilogue):
    s, d = x.shape
    ncols = w.shape[-1]
    tm = min(NMM_TM, s)
    return pl.pallas_call(
        functools.partial(_nmm_body, n_extra=len(extra), epilogue=epilogue),
        grid=(s // tm, ncols // NMM_TN),
        in_specs=[
            pl.BlockSpec((tm, d), lambda i, n: (i, 0)),
            pl.BlockSpec((1, d), lambda i, n: (0, 0)),
            pl.BlockSpec((None, d, NMM_TN), lambda i, n: (layer, 0, n)),
        ] + list(extra_specs),
        out_specs=out_specs,
        out_shape=out_shapes,
        scratch_shapes=[pltpu.VMEM((tm, d), BF16)],
        compiler_params=_params("parallel", "arbitrary"),
        name="norm_matmul",
    )(x, gain.reshape(1, d), w, *extra)


def _head_norm(y, gain, scale):
    parts = []
    for c in range(NMM_TN // A_HEAD_DIM):
        z = y[:, c * A_HEAD_DIM:(c + 1) * A_HEAD_DIM]
        parts.append(z * _rms_scale(z) * (gain * scale))
    return jnp.concatenate(parts, axis=1)


def _dsa_in_epilogue(n, y, extra, out_refs):
    qg_ref, kg_ref, ikg_ref = extra
    main_ref, iw_ref = out_refs
    q_tiles = A_Q_COLS // NMM_TN
    k_tile = q_tiles
    v_tile = k_tile + 1
    iq_tile0 = v_tile + 1
    idx_tile = A_MAIN_COLS // NMM_TN

    @pl.when(n < q_tiles)
    def _():
        main_ref[...] = _head_norm(y, qg_ref[...], A_HEAD_DIM ** -0.5).astype(BF16)

    @pl.when(n == k_tile)
    def _():
        main_ref[...] = _head_norm(y, kg_ref[...], 1.0).astype(BF16)

    @pl.when(n == v_tile)
    def _():
        main_ref[...] = y.astype(BF16)

    @pl.when(jnp.logical_and(n >= iq_tile0, n < idx_tile))
    def _():
        main_ref[...] = (y * IDX_DIM ** -0.5).astype(BF16)

    @pl.when(n == idx_tile)
    def _():
        z = y[:, :LANES]
        lane = lax.broadcasted_iota(I32, z.shape, 1)
        ms = jnp.sum(jnp.where(lane < IDX_DIM, z * z, 0.0), axis=-1, keepdims=True) * (1.0 / IDX_DIM)
        ikn = z * lax.rsqrt(ms + EPS) * ikg_ref[...]
        main_ref[...] = jnp.concatenate(
            [ikn, jnp.zeros((y.shape[0], NMM_TN - LANES), F32)], axis=1).astype(BF16)
        iw_ref[...] = z * IDX_HEADS ** -0.5


def _dsa_in_proj(x, gain, w_cat, layer, q_gain, k_gain, ik_gain_padded):
    s = x.shape[0]
    tm = min(NMM_TM, s)
    ncols = w_cat.shape[-1]
    vec = lambda i, n: (0, 0)
    return _norm_matmul(
        x, gain, w_cat, layer,
        extra=(q_gain.reshape(1, A_HEAD_DIM), k_gain.reshape(1, A_HEAD_DIM), ik_gain_padded),
        extra_specs=[pl.BlockSpec((1, A_HEAD_DIM), vec), pl.BlockSpec((1, A_HEAD_DIM), vec),
                     pl.BlockSpec((1, LANES), vec)],
        out_shapes=(jax.ShapeDtypeStruct((s, ncols), BF16), jax.ShapeDtypeStruct((s, LANES), F32)),
        out_specs=(pl.BlockSpec((tm, NMM_TN), lambda i, n: (i, n)),
                   pl.BlockSpec((tm, LANES), lambda i, n: (i, 0))),
        epilogue=_dsa_in_epilogue,
    )


def _identity_epilogue(n, y, extra, out_refs):
    out_refs[0][...] = y


def _hgrn_in_proj(x, gain, w, layer):
    s = x.shape[0]
    tm = min(NMM_TM, s)
    return _norm_matmul(
        x, gain, w, layer, extra=(), extra_specs=[],
        out_shapes=jax.ShapeDtypeStruct((s, w.shape[-1]), F32),
        out_specs=pl.BlockSpec((tm, NMM_TN), lambda i, n: (i, n)),
        epilogue=_identity_epilogue,
    )


def _out_proj_body(a_ref, w_ref, r_ref, o_ref):
    o_ref[...] = r_ref[...] + jnp.dot(a_ref[...], w_ref[...], preferred_element_type=F32)


def _out_proj(a, w, layer, res):
    s, k = a.shape
    d = w.shape[-1]
    tm = min(OUT_TM, s)
    return pl.pallas_call(
        _out_proj_body,
        grid=(s // tm,),
        in_specs=[
            pl.BlockSpec((tm, k), lambda i: (i, 0)),
            _resident((None, k, d), lambda i: (layer, 0, 0)),
            pl.BlockSpec((tm, d), lambda i: (i, 0)),
        ],
        out_specs=pl.BlockSpec((tm, d), lambda i: (i, 0)),
        out_shape=jax.ShapeDtypeStruct((s, d), F32),
        compiler_params=_params("parallel"),
        name="out_proj",
    )(a, w, res)


def _t5_bucket_table():
    exact = REL_BUCKETS // 2
    d = np.arange(2 * QB)
    df = np.maximum(d, 1).astype(np.float64)
    val = np.log(df / exact) / math.log(REL_MAX_DIST / exact) * (REL_BUCKETS - exact)
    big = d >= exact
    frac = np.abs(val[big] - np.round(val[big]))
    assert np.all((frac > 1e-4) | (frac == 0.0))
    large = np.minimum(exact + np.floor(val + 1e-9).astype(np.int64), REL_BUCKETS - 1)
    return np.where(d < exact, d, large).astype(np.int32)


def _near_bucket_tiles():
    table = _t5_bucket_table()
    r = np.arange(QB)[:, None]
    t = np.arange(QB)[None, :]
    tiles = [table[np.maximum(n * QB + t - r, 0)] for n in range(2)]
    return np.stack(tiles).astype(np.int32)


def _bias_body(rel_ref, bkt_ref, o_ref):
    j = pl.program_id(1)
    bkt = bkt_ref[0]
    for g in range(A_GROUP):
        h = j * A_GROUP + g
        far = rel_ref[REL_BUCKETS - 1, h]
        acc = jnp.zeros(bkt.shape, F32)
        for b in range(REL_BUCKETS):
            acc = jnp.where(bkt == b, rel_ref[b, h] - far, acc)
        o_ref[0, 0, :, g * QB:(g + 1) * QB] = acc


def _near_bias(rel_bias):
    bkt = jnp.asarray(_near_bucket_tiles())
    return pl.pallas_call(
        _bias_body,
        grid=(2, A_KV_HEADS),
        in_specs=[
            pl.BlockSpec(memory_space=pltpu.SMEM),
            pl.BlockSpec((1, QB, QB), lambda n, j: (n, 0, 0)),
        ],
        out_specs=pl.BlockSpec((1, 1, QB, A_GROUP * QB), lambda n, j: (n, j, 0, 0)),
        out_shape=jax.ShapeDtypeStruct((2, A_KV_HEADS, QB, A_GROUP * QB), F32),
        compiler_params=_params("arbitrary", "arbitrary"),
        name="near_bias",
    )(rel_bias, bkt)


def _attn_body(q_ref, iq_ref, iw_ref, k_ref, ik_ref, vt_ref, bias_ref, o_ref,
               iqt_ref, qt_ref, sc_ref, st_ref, acc_ref, m_ref, *, topk):
    i = pl.program_id(0)
    gq = A_GROUP * QB
    POS_BITS = (sc_ref.shape[0] - 1).bit_length()

    iqt = iq_ref[...].astype(F32).T
    iqt_ref[...] = jnp.zeros_like(iqt_ref)
    for h in range(IDX_HEADS):
        iqt_ref[0:IDX_DIM, h * QB:(h + 1) * QB] = iqt[h * IDX_DIM:(h + 1) * IDX_DIM, :].astype(BF16)
    qt = q_ref[...].astype(F32).T
    eye = (lax.broadcasted_iota(I32, (QB, QB), 0) == lax.broadcasted_iota(I32, (QB, QB), 1)).astype(BF16)
    for h in range(A_HEADS):
        qt_ref[0:A_HEAD_DIM, h * QB:(h + 1) * QB] = qt[h * A_HEAD_DIM:(h + 1) * A_HEAD_DIM, :].astype(BF16)
        qt_ref[A_HEAD_DIM:, h * QB:(h + 1) * QB] = eye
    iwt = iw_ref[...].T

    n_chunks = (i * QB + QB + SCORE_CHUNK - 1) // SCORE_CHUNK

    def score_chunk(c, carry):
        row0 = pl.multiple_of(c * SCORE_CHUNK, SCORE_CHUNK)
        ikc = ik_ref[pl.ds(row0, SCORE_CHUNK), :]
        sc = jnp.zeros((SCORE_CHUNK, QB), F32)
        for hp in range(IDX_HEADS // 2):
            raw = jnp.dot(ikc, iqt_ref[:, hp * 2 * QB:(hp + 1) * 2 * QB], preferred_element_type=F32)
            for u in range(2):
                h = 2 * hp + u
                sc = sc + jnp.maximum(raw[:, u * QB:(u + 1) * QB], 0.0) * iwt[IDX_DIM + h:IDX_DIM + h + 1, :]
        key_pos = row0 + lax.broadcasted_iota(I32, sc.shape, 0)
        q_pos = i * QB + lax.broadcasted_iota(I32, sc.shape, 1)
        bits = pltpu.bitcast(sc, I32)
        skey = jnp.where(bits >= 0, bits, bits ^ 0x7FFFFFFF)
        sc_ref[pl.ds(row0, SCORE_CHUNK), :] = jnp.where(key_pos <= q_pos, skey, INT_MIN)
        return carry

    lax.fori_loop(0, n_chunks, score_chunk, 0)

    n_count = n_chunks * (SCORE_CHUNK // COUNT_ROWS)

    def count_keys(pred):
        def body(r, acc):
            row0 = pl.multiple_of(r * COUNT_ROWS, COUNT_ROWS)
            hit = jnp.where(pred(sc_ref[pl.ds(row0, COUNT_ROWS), :], row0), 1, 0).astype(I32)
            return acc + jnp.sum(hit.reshape(COUNT_ROWS // 8, 8, QB), axis=0)
        acc = lax.fori_loop(0, n_count, body, jnp.zeros((8, QB), I32))
        return jnp.sum(acc, axis=0, keepdims=True)

    def bit_step(it, ans):
        bit = 31 - it
        cand = jnp.where(it == 0, jnp.zeros_like(ans), ans | (jnp.int32(1) << bit))
        return jnp.where(count_keys(lambda blk, row0: blk >= cand) >= topk, cand, ans)

    ans = lax.fori_loop(0, 32, bit_step, jnp.full((1, QB), INT_MIN, I32))
    thr = jnp.maximum(ans, INT_MIN + 1)

    n_ge = count_keys(lambda blk, row0: blk >= thr)

    @pl.when(jnp.max(n_ge) > topk)
    def _():
        need = topk - count_keys(lambda blk, row0: blk > thr)

        def key_pos_of(blk, row0):
            return row0 + lax.broadcasted_iota(I32, blk.shape, 0)

        def pos_step(it, last):
            cand = last | (jnp.int32(1) << (POS_BITS - 1 - it))
            n_before = count_keys(lambda blk, row0: jnp.logical_and(blk == thr, key_pos_of(blk, row0) < cand))
            return jnp.where(n_before < need, cand, last)

        last = lax.fori_loop(0, POS_BITS, pos_step, jnp.zeros((1, QB), I32))

        def demote(r, carry):
            row0 = pl.multiple_of(r * COUNT_ROWS, COUNT_ROWS)
            blk = sc_ref[pl.ds(row0, COUNT_ROWS), :]
            late_tie = jnp.logical_and(blk == thr, key_pos_of(blk, row0) > last)
            sc_ref[pl.ds(row0, COUNT_ROWS), :] = jnp.where(late_tie, thr - 1, blk)
            return carry

        lax.fori_loop(0, n_count, demote, 0)

    m_ref[...] = jnp.full_like(m_ref, M_INIT)
    acc_ref[...] = jnp.zeros_like(acc_ref)
    heads = range(A_KV_HEADS)

    def logits(c, slot):
        row0 = pl.multiple_of(c * KB, KB)
        sel = sc_ref[pl.ds(row0, KB), :] >= thr
        mask_cols = jnp.where(sel, 0.0, MASKED_LOGIT).astype(BF16)
        kblk = k_ref[pl.ds(row0, KB), :]
        for j in heads:
            kaug = jnp.concatenate([kblk[:, j * A_HEAD_DIM:(j + 1) * A_HEAD_DIM], mask_cols], axis=1)
            st_ref[slot, j] = jnp.dot(kaug, qt_ref[:, j * gq:(j + 1) * gq], preferred_element_type=F32)

    def softmax_pv(c, slot, near):
        ps, alphas = [], []
        for j in heads:
            st = st_ref[slot, j]
            if near != (None, None):
                halves = [st[0:QB], st[QB:KB]]
                halves = [hv if n is None else hv + bias_ref[n, j] for hv, n in zip(halves, near)]
                st = jnp.concatenate(halves, axis=0)
            m_old = m_ref[j]
            m_new = jnp.maximum(m_old, jnp.max(st, axis=0, keepdims=True))
            m_ref[j] = m_new
            ps.append(jnp.exp(st - m_new).astype(BF16))
            alphas.append(jnp.exp(m_old - m_new))
        vblk = vt_ref[c]
        pvs = [jnp.dot(vblk[j * VT_ROWS:(j + 1) * VT_ROWS, :], ps[j], preferred_element_type=F32)
               for j in heads]
        for j in heads:
            acc_ref[j] = acc_ref[j] * alphas[j] + pvs[j]

    far = (None, None)
    n_far = jnp.maximum(lax.shift_right_arithmetic(i - 1, 1), 0)
    i_odd = (i & 1) == 1

    @pl.when(i >= 1)
    def _():
        logits(0, 0)

    @pl.when(i == 0)
    def _():
        logits(0, 1)

    def far_pair(pp, carry):
        c0 = 2 * pp
        logits(c0 + 1, 1)
        softmax_pv(c0, 0, far)
        logits(c0 + 2, 0)
        softmax_pv(c0 + 1, 1, far)
        return carry

    lax.fori_loop(0, lax.shift_right_arithmetic(n_far, 1), far_pair, 0)

    @pl.when((n_far & 1) == 1)
    def _():
        softmax_pv(n_far - 1, 0, far)
        logits(n_far, 0)

    @pl.when(i_odd)
    def _():
        softmax_pv(n_far, 0, (1, 0))

    @pl.when(jnp.logical_and(jnp.logical_not(i_odd), i >= 2))
    def _():
        logits(n_far + 1, 1)
        softmax_pv(n_far, 0, (None, 1))

    @pl.when(jnp.logical_not(i_odd))
    def _():
        softmax_pv(lax.shift_right_arithmetic(i, 1), 1, (0, None))

    for j in heads:
        acc = acc_ref[j]
        out_t = acc[0:A_HEAD_DIM, :] * (1.0 / acc[A_HEAD_DIM:A_HEAD_DIM + 1, :])
        for g in range(A_GROUP):
            h = j * A_GROUP + g
            o_ref[:, h * A_HEAD_DIM:(h + 1) * A_HEAD_DIM] = out_t[:, g * QB:(g + 1) * QB].T.astype(o_ref.dtype)


def _dsa_attention(main, iw, vt3, bias, topk):
    s = main.shape[0]
    nb = s // QB
    q_blk = A_Q_COLS // A_Q_COLS
    del q_blk
    return pl.pallas_call(
        functools.partial(_attn_body, topk=topk),
        grid=(nb,),
        in_specs=[
            pl.BlockSpec((QB, A_Q_COLS), lambda i: (i, 0)),
            pl.BlockSpec((QB, A_IQ_COLS), lambda i: (i, (A_Q_COLS + 2 * A_KV_COLS) // A_IQ_COLS)),
            pl.BlockSpec((QB, LANES), lambda i: (i, 0)),
            _resident((s, A_KV_COLS), lambda i: (0, A_Q_COLS // A_KV_COLS)),
            _resident((s, LANES), lambda i: (0, A_MAIN_COLS // LANES)),
            _resident((s // KB, A_KV_HEADS * VT_ROWS, KB), lambda i: (0, 0, 0)),
            _resident((2, A_KV_HEADS, QB, A_GROUP * QB), lambda i: (0, 0, 0, 0)),
        ],
        out_specs=pl.BlockSpec((QB, A_Q_COLS), lambda i: (i, 0)),
        out_shape=jax.ShapeDtypeStruct((s, A_Q_COLS), BF16),
        scratch_shapes=[
            pltpu.VMEM((LANES, IDX_HEADS * QB), BF16),
            pltpu.VMEM((A_HEAD_DIM + QB, A_HEADS * QB), BF16),
            pltpu.VMEM((s, QB), I32),
            pltpu.VMEM((2, A_KV_HEADS, KB, A_GROUP * QB), F32),
            pltpu.VMEM((A_KV_HEADS, VT_ROWS, A_GROUP * QB), F32),
            pltpu.VMEM((A_KV_HEADS, 1, A_GROUP * QB), F32),
        ],
        compiler_params=_params("arbitrary"),
        name="dsa_attention",
    )(main, main, iw, main, main, vt3, bias)


def _dsa_mixer(x, gain, w_cat, w_out, layer, q_gain, k_gain, ik_gain_padded, bias):
    s = x.shape[0]
    topk = min(TOPK_MAX, s // 4)
    main, iw = _dsa_in_proj(x, gain, w_cat, layer, q_gain, k_gain, ik_gain_padded)
    v = main[:, A_Q_COLS + A_KV_COLS:A_Q_COLS + 2 * A_KV_COLS]
    vt4 = v.reshape(s // KB, KB, A_KV_HEADS, A_HEAD_DIM).transpose(0, 2, 3, 1)
    tail = jnp.zeros((s // KB, A_KV_HEADS, VT_ROWS - A_HEAD_DIM, KB), BF16).at[:, :, 0, :].set(1.0)
    vt3 = jnp.concatenate([vt4, tail], axis=2).reshape(s // KB, A_KV_HEADS * VT_ROWS, KB)
    o = _dsa_attention(main, iw, vt3, bias, topk)
    return _out_proj(o, w_out, layer, x)


def _hgrn_sum_matrices():
    c = HG_CHUNK
    t = np.arange(c)[:, None]
    s = np.arange(c)[None, :]
    mats = []
    for l in range(1, HG_LEVELS + 1):
        blk = 1 << l
        m = (t // blk) * blk + blk // 2
        upper = t >= m
        mats.append(np.where(upper, (s >= m) & (s <= t), (s > t) & (s <= m - 1)))
    mats.append(s <= t)
    mats.append(s > t)
    return np.concatenate(mats, axis=0).astype(np.float32)


def _hgrn_split_levels():
    t = np.arange(HG_CHUNK)[:, None]
    s = np.arange(HG_CHUNK)[None, :]
    lvl = np.zeros((HG_CHUNK, HG_CHUNK), np.int32)
    for l in range(1, HG_LEVELS + 1):
        lvl[(s < t) & ((t >> l) == (s >> l)) & (((t >> (l - 1)) & 1) == 1) & (((s >> (l - 1)) & 1) == 0)] = l
    assert np.all((lvl > 0) == (s < t))
    return lvl


def _hgrn_body(q_ref, f_ref, i_ref, g_ref, lbl_ref, og_ref, gm_ref, split_ref, o_ref, st_ref, *, layer):
    c = pl.program_id(1)
    ch = HG_CHUNK

    @pl.when(c == 0)
    def _():
        st_ref[...] = jnp.zeros_like(st_ref)

    logits = lbl_ref[...]
    e = jnp.exp(logits - jnp.max(logits, axis=0, keepdims=True))
    p = e / jnp.sum(e, axis=0, keepdims=True)
    csum = p[0:1, :]
    for r in range(1, layer + 1):
        csum = csum + p[r:r + 1, :]
    lb = csum - p[0:1, :]

    fgate = lb + (1.0 - lb) * jax.nn.sigmoid(f_ref[...])
    kin = 1.0 - fgate
    logf = jnp.log(fgate)
    logf_hi = logf.astype(BF16)
    logf_lo = (logf - logf_hi.astype(F32)).astype(BF16)
    logf_split = jnp.concatenate([logf_hi, logf_lo], axis=0)
    decs = [jnp.dot(gm_ref[r * ch:(r + 1) * ch, :], logf_split, preferred_element_type=F32)
            for r in range(HG_LEVELS + 2)]

    row = lax.broadcasted_iota(I32, (ch, B_KEY_DIM), 0)
    split = split_ref[...]
    nt = (((1,), (1,)), ((), ()))
    heads = range(HG_HEADS_PER_STEP)
    levels = range(1, HG_LEVELS + 1)
    sls = [slice(hh * B_KEY_DIM, (hh + 1) * B_KEY_DIM) for hh in heads]

    qs = [q_ref[:, sl] * (B_KEY_DIM ** -0.5) for sl in sls]
    ks = [kin[:, sl] for sl in sls]
    vs = [i_ref[:, sl] for sl in sls]
    b_incl = [decs[HG_LEVELS][:, sl] for sl in sls]
    b_rest = [decs[HG_LEVELS + 1][:, sl] for sl in sls]
    states = [st_ref[hh] for hh in heads]

    zs = []
    for hh in heads:
        for l in levels:
            upper = (row & (1 << (l - 1))) != 0
            ex = jnp.exp(decs[l - 1][:, sls[hh]])
            zs.append((jnp.where(upper, qs[hh], ks[hh]) * ex).astype(BF16))
    prods = [lax.dot_general(z, z, nt, preferred_element_type=F32) for z in zs]
    o_inter = [lax.dot_general((qs[hh] * jnp.exp(b_incl[hh])).astype(BF16), states[hh].astype(BF16), nt,
                               preferred_element_type=F32) for hh in heads]
    new_kv = [jnp.dot(vs[hh].T.astype(BF16), (ks[hh] * jnp.exp(b_rest[hh])).astype(BF16),
                      preferred_element_type=F32) for hh in heads]
    a_mats = []
    for hh in heads:
        a = jnp.zeros((ch, ch), F32)
        for l in levels:
            a = jnp.where(split == l, prods[hh * HG_LEVELS + l - 1], a)
        a_mats.append(a.astype(BF16))
    o_intra = [jnp.dot(a_mats[hh], vs[hh].astype(BF16), preferred_element_type=F32) for hh in heads]

    for hh in heads:
        sl = sls[hh]
        st_ref[hh] = states[hh] * jnp.exp(b_incl[hh][ch - 1:ch, :]) + new_kv[hh]
        o = o_inter[hh] + o_intra[hh] + jnp.sum(qs[hh] * ks[hh], axis=-1, keepdims=True) * vs[hh]
        on = o * _rms_scale(o) * og_ref[:, sl]
        gate = g_ref[:, sl]
        o_ref[:, sl] = (on * (gate * jax.nn.sigmoid(gate))).astype(o_ref.dtype)


def _hgrn_core(proj, lb_logits, o_gain, layer):
    s = proj.shape[0]
    w = HG_HEADS_PER_STEP * B_KEY_DIM
    nhb = D_MODEL // w
    gm = jnp.asarray(np.tile(_hgrn_sum_matrices(), (1, 2)), dtype=BF16)
    split = jnp.asarray(_hgrn_split_levels())
    col = lambda off: (lambda hb, c: (c, off * nhb + hb))
    return pl.pallas_call(
        functools.partial(_hgrn_body, layer=layer),
        grid=(nhb, s // HG_CHUNK),
        in_specs=[
            pl.BlockSpec((HG_CHUNK, w), col(0)),
            pl.BlockSpec((HG_CHUNK, w), col(1)),
            pl.BlockSpec((HG_CHUNK, w), col(2)),
            pl.BlockSpec((HG_CHUNK, w), col(3)),
            pl.BlockSpec((DEPTH, w), lambda hb, c: (0, hb)),
            pl.BlockSpec((1, w), lambda hb, c: (0, hb)),
            pl.BlockSpec(gm.shape, lambda hb, c: (0, 0)),
            pl.BlockSpec(split.shape, lambda hb, c: (0, 0)),
        ],
        out_specs=pl.BlockSpec((HG_CHUNK, w), lambda hb, c: (c, hb)),
        out_shape=jax.ShapeDtypeStruct((s, D_MODEL), BF16),
        scratch_shapes=[pltpu.VMEM((HG_HEADS_PER_STEP, B_KEY_DIM, B_KEY_DIM), F32)],
        compiler_params=_params("parallel", "arbitrary"),
        name="hgrn_core",
    )(proj, proj, proj, proj, lb_logits, o_gain.reshape(1, D_MODEL), gm, split)


def _hgrn_mixer(x, gain, w_in, w_out, mixer_idx, lb_logits, o_gain, layer):
    proj = _hgrn_in_proj(x, gain, w_in, mixer_idx)
    o = _hgrn_core(proj, lb_logits, o_gain, layer)
    return _out_proj(o, w_out, mixer_idx, x)


def kernel(x, norm_gains, ffn_w_gate, ffn_w_up, ffn_w_down, dsa_w_in, dsa_w_out, dsa_q_gain, dsa_k_gain,
           dsa_idx_k_gain, rel_bias, hgrn_w_in, hgrn_w_out, hgrn_lb_logits, hgrn_o_gain):
    assert x.shape[0] == 1 and x.shape[2] == D_MODEL and x.shape[1] % FFN_TM == 0
    h = x.reshape(x.shape[1], D_MODEL)
    wg = ffn_w_gate.astype(BF16)
    wu = ffn_w_up.astype(BF16)
    wd = ffn_w_down.astype(BF16)
    pad_cols = NMM_TN - (A_IN_COLS - A_MAIN_COLS)
    dsa_w_cat = jnp.pad(dsa_w_in, ((0, 0), (0, 0), (0, pad_cols))).astype(BF16)
    dsa_wo = dsa_w_out.astype(BF16)
    ik_gain = jnp.pad(dsa_idx_k_gain, ((0, 0), (0, LANES - IDX_DIM)))
    hg_wi = hgrn_w_in.astype(BF16)
    hg_wo = hgrn_w_out.astype(BF16)
    bias = _near_bias(rel_bias)

    for layer in range(DEPTH):
        j = layer // N_MIXERS
        h = _ffn(h, norm_gains[layer, 0], wg, wu, wd, lead=(layer, 0))
        if layer % N_MIXERS == 0:
            h = _dsa_mixer(h, norm_gains[layer, 1], dsa_w_cat, dsa_wo, j, dsa_q_gain[j], dsa_k_gain[j],
                           ik_gain[j:j + 1], bias)
        else:
            h = _hgrn_mixer(h, norm_gains[layer, 1], hg_wi, hg_wo, j, hgrn_lb_logits, hgrn_o_gain[j], layer)
        h = _ffn(h, norm_gains[layer, 2], wg, wu, wd, lead=(layer, 1))
    return h.reshape(x.shape)
```

```python
import functools
import math

import numpy as np
import jax
import jax.numpy as jnp
from jax import lax
from jax.experimental import pallas as pl
from jax.experimental.pallas import tpu as pltpu

F32 = jnp.float32
BF16 = jnp.bfloat16
I32 = jnp.int32

D_MODEL = 2048
DEPTH = 4
N_MIXERS = 2
A_HEADS = 16
A_HEAD_DIM = 128
A_KV_HEADS = 4
A_GROUP = A_HEADS // A_KV_HEADS
IDX_HEADS = 16
IDX_DIM = 64
TOPK_MAX = 256
REL_BUCKETS = 32
REL_MAX_DIST = 128
B_KEY_DIM = 128
B_HEADS = D_MODEL // B_KEY_DIM
D_FF = 5632
EPS = 1e-6

A_Q_COLS = A_HEADS * A_HEAD_DIM
A_KV_COLS = A_KV_HEADS * A_HEAD_DIM
A_IQ_COLS = IDX_HEADS * IDX_DIM
A_MAIN_COLS = A_Q_COLS + 2 * A_KV_COLS + A_IQ_COLS
A_IN_COLS = A_MAIN_COLS + IDX_DIM + IDX_HEADS

LANES = 128
V7X_VMEM_BYTES = 64 * 1024 * 1024
VMEM_LIMIT_BYTES = 56 * 1024 * 1024

INT_MIN = -(2 ** 31)
LOG2_E = math.log2(math.e)
MASKED_LOGIT = -1e30
M_INIT = -1e20
VT_ROWS = A_HEAD_DIM + 16

FFN_TM = 1024
FFN_TF = 512
NMM_TM = 1024
NMM_TN = 512
OUT_TM = 512
QB = 128
KB = 2 * QB
SCORE_CHUNK = 512
PLANE_KEYS = 256
COUNT_ROWS = 512
HG_CHUNK = 128
HG_HEADS_PER_STEP = 4
HG_CHUNKS_PER_STEP = 2
HG_LEVELS = 7


def _params(*semantics):
    return pltpu.CompilerParams(dimension_semantics=semantics, vmem_limit_bytes=VMEM_LIMIT_BYTES)


def _resident(shape, index_map):
    return pl.BlockSpec(shape, index_map, pipeline_mode=pl.Buffered(1))


def _rms_scale(x):
    return lax.rsqrt(jnp.mean(x * x, axis=-1, keepdims=True) + EPS)


def _ffn_body(x_ref, g_ref, wg_ref, wu_ref, wd_ref, o_ref, h_ref):
    j = pl.program_id(1)

    @pl.when(j == 0)
    def _():
        x = x_ref[...]
        h_ref[...] = (x * _rms_scale(x) * g_ref[...]).astype(BF16)
        o_ref[...] = jnp.zeros_like(o_ref)

    h = h_ref[...]
    gate = jnp.dot(h, wg_ref[...], preferred_element_type=F32)
    up = jnp.dot(h, wu_ref[...], preferred_element_type=F32)
    act = (gate * jax.nn.sigmoid(gate) * up).astype(BF16)
    o_ref[...] += jnp.dot(act, wd_ref[...], preferred_element_type=F32)

    @pl.when(j == pl.num_programs(1) - 1)
    def _():
        o_ref[...] = x_ref[...] + 0.5 * o_ref[...]


def _ffn(x, gain, wg, wu, wd, lead=()):
    s, d = x.shape
    f = wg.shape[-1]
    tm = min(FFN_TM, s)
    squeeze = (None,) * len(lead)
    return pl.pallas_call(
        _ffn_body,
        grid=(s // tm, f // FFN_TF),
        in_specs=[
            pl.BlockSpec((tm, d), lambda i, j: (i, 0)),
            pl.BlockSpec((1, d), lambda i, j: (0, 0)),
            pl.BlockSpec(squeeze + (d, FFN_TF), lambda i, j: lead + (0, j)),
            pl.BlockSpec(squeeze + (d, FFN_TF), lambda i, j: lead + (0, j)),
            pl.BlockSpec(squeeze + (FFN_TF, d), lambda i, j: lead + (j, 0)),
        ],
        out_specs=pl.BlockSpec((tm, d), lambda i, j: (i, 0)),
        out_shape=jax.ShapeDtypeStruct((s, d), F32),
        scratch_shapes=[pltpu.VMEM((tm, d), BF16)],
        compiler_params=_params("parallel", "arbitrary"),
        name="ffn",
    )(x, gain.reshape(1, d), wg, wu, wd)


def _nmm_body(*refs, n_extra, epilogue):
    x_ref, g_ref, w_ref = refs[:3]
    extra = refs[3:3 + n_extra]
    out_refs = refs[3 + n_extra:-1]
    h_ref = refs[-1]
    n = pl.program_id(1)

    @pl.when(n == 0)
    def _():
        x = x_ref[...]
        h_ref[...] = (x * _rms_scale(x) * g_ref[...]).astype(BF16)

    y = jnp.dot(h_ref[...], w_ref[...], preferred_element_type=F32)
    epilogue(n, y, extra, out_refs)


def _norm_matmul(x, gain, w, layer, extra, extra_specs, out_shapes, out_specs, epilogue):
    s, d = x.shape
    ncols = w.shape[-1]
    tm = min(NMM_TM, s)
    return pl.pallas_call(
        functools.partial(_nmm_body, n_extra=len(extra), epilogue=epilogue),
        grid=(s // tm, ncols // NMM_TN),
        in_specs=[
            pl.BlockSpec((tm, d), lambda i, n: (i, 0)),
            pl.BlockSpec((1, d), lambda i, n: (0, 0)),
            pl.BlockSpec((None, d, NMM_TN), lambda i, n: (layer, 0, n)),
        ] + list(extra_specs),
        out_specs=out_specs,
        out_shape=out_shapes,
        scratch_shapes=[pltpu.VMEM((tm, d), BF16)],
        compiler_params=_params("parallel", "arbitrary"),
        name="norm_matmul",
    )(x, gain.reshape(1, d), w, *extra)


def _head_norm(y, gain, scale):
    parts = []
    for c in range(NMM_TN // A_HEAD_DIM):
        z = y[:, c * A_HEAD_DIM:(c + 1) * A_HEAD_DIM]
        parts.append(z * _rms_scale(z) * (gain * scale))
    return jnp.concatenate(parts, axis=1)


def _dsa_in_epilogue(n, y, extra, out_refs):
    qg_ref, kg_ref, ikg_ref = extra
    main_ref, iw_ref = out_refs
    q_tiles = A_Q_COLS // NMM_TN
    k_tile = q_tiles
    v_tile = k_tile + 1
    iq_tile0 = v_tile + 1
    idx_tile = A_MAIN_COLS // NMM_TN

    @pl.when(n < q_tiles)
    def _():
        main_ref[...] = _head_norm(y, qg_ref[...], A_HEAD_DIM ** -0.5 * LOG2_E).astype(BF16)

    @pl.when(n == k_tile)
    def _():
        main_ref[...] = _head_norm(y, kg_ref[...], 1.0).astype(BF16)

    @pl.when(n == v_tile)
    def _():
        main_ref[...] = y.astype(BF16)

    @pl.when(jnp.logical_and(n >= iq_tile0, n < idx_tile))
    def _():
        main_ref[...] = (y * IDX_DIM ** -0.5).astype(BF16)

    @pl.when(n == idx_tile)
    def _():
        z = y[:, :LANES]
        lane = lax.broadcasted_iota(I32, z.shape, 1)
        ms = jnp.sum(jnp.where(lane < IDX_DIM, z * z, 0.0), axis=-1, keepdims=True) * (1.0 / IDX_DIM)
        ikn = z * lax.rsqrt(ms + EPS) * ikg_ref[...]
        main_ref[...] = jnp.concatenate(
            [ikn, jnp.zeros((y.shape[0], NMM_TN - LANES), F32)], axis=1).astype(BF16)
        iw_ref[...] = z * IDX_HEADS ** -0.5


def _dsa_in_proj(x, gain, w_cat, layer, q_gain, k_gain, ik_gain_padded):
    s = x.shape[0]
    tm = min(NMM_TM, s)
    ncols = w_cat.shape[-1]
    vec = lambda i, n: (0, 0)
    return _norm_matmul(
        x, gain, w_cat, layer,
        extra=(q_gain.reshape(1, A_HEAD_DIM), k_gain.reshape(1, A_HEAD_DIM), ik_gain_padded),
        extra_specs=[pl.BlockSpec((1, A_HEAD_DIM), vec), pl.BlockSpec((1, A_HEAD_DIM), vec),
                     pl.BlockSpec((1, LANES), vec)],
        out_shapes=(jax.ShapeDtypeStruct((s, ncols), BF16), jax.ShapeDtypeStruct((s, LANES), F32)),
        out_specs=(pl.BlockSpec((tm, NMM_TN), lambda i, n: (i, n)),
                   pl.BlockSpec((tm, LANES), lambda i, n: (i, 0))),
        epilogue=_dsa_in_epilogue,
    )


def _identity_epilogue(n, y, extra, out_refs):
    out_refs[0][...] = y


def _hgrn_in_proj(x, gain, w, layer):
    s = x.shape[0]
    tm = min(NMM_TM, s)
    return _norm_matmul(
        x, gain, w, layer, extra=(), extra_specs=[],
        out_shapes=jax.ShapeDtypeStruct((s, w.shape[-1]), F32),
        out_specs=pl.BlockSpec((tm, NMM_TN), lambda i, n: (i, n)),
        epilogue=_identity_epilogue,
    )


def _out_proj_body(a_ref, w_ref, r_ref, o_ref):
    o_ref[...] = r_ref[...] + jnp.dot(a_ref[...], w_ref[...], preferred_element_type=F32)


def _out_proj(a, w, layer, res):
    s, k = a.shape
    d = w.shape[-1]
    tm = min(OUT_TM, s)
    return pl.pallas_call(
        _out_proj_body,
        grid=(s // tm,),
        in_specs=[
            pl.BlockSpec((tm, k), lambda i: (i, 0)),
            _resident((None, k, d), lambda i: (layer, 0, 0)),
            pl.BlockSpec((tm, d), lambda i: (i, 0)),
        ],
        out_specs=pl.BlockSpec((tm, d), lambda i: (i, 0)),
        out_shape=jax.ShapeDtypeStruct((s, d), F32),
        compiler_params=_params("parallel"),
        name="out_proj",
    )(a, w, res)


def _t5_bucket_table():
    exact = REL_BUCKETS // 2
    d = np.arange(2 * QB)
    df = np.maximum(d, 1).astype(np.float64)
    val = np.log(df / exact) / math.log(REL_MAX_DIST / exact) * (REL_BUCKETS - exact)
    big = d >= exact
    frac = np.abs(val[big] - np.round(val[big]))
    assert np.all((frac > 1e-4) | (frac == 0.0))
    large = np.minimum(exact + np.floor(val + 1e-9).astype(np.int64), REL_BUCKETS - 1)
    return np.where(d < exact, d, large).astype(np.int32)


def _near_bucket_tiles():
    table = _t5_bucket_table()
    r = np.arange(QB)[:, None]
    t = np.arange(QB)[None, :]
    tiles = [table[np.maximum(n * QB + t - r, 0)] for n in range(2)]
    return np.stack(tiles).astype(np.int32)


def _bias_body(rel_ref, bkt_ref, o_ref):
    j = pl.program_id(1)
    bkt = bkt_ref[0]
    for g in range(A_GROUP):
        h = j * A_GROUP + g
        far = rel_ref[REL_BUCKETS - 1, h]
        acc = jnp.zeros(bkt.shape, F32)
        for b in range(REL_BUCKETS):
            acc = jnp.where(bkt == b, (rel_ref[b, h] - far) * LOG2_E, acc)
        o_ref[0, 0, :, g * QB:(g + 1) * QB] = acc


def _near_bias(rel_bias):
    bkt = jnp.asarray(_near_bucket_tiles())
    return pl.pallas_call(
        _bias_body,
        grid=(2, A_KV_HEADS),
        in_specs=[
            pl.BlockSpec(memory_space=pltpu.SMEM),
            pl.BlockSpec((1, QB, QB), lambda n, j: (n, 0, 0)),
        ],
        out_specs=pl.BlockSpec((1, 1, QB, A_GROUP * QB), lambda n, j: (n, j, 0, 0)),
        out_shape=jax.ShapeDtypeStruct((2, A_KV_HEADS, QB, A_GROUP * QB), F32),
        compiler_params=_params("arbitrary", "arbitrary"),
        name="near_bias",
    )(rel_bias, bkt)


def _bit_transpose32(words):
    a = list(words)
    for j, mask in ((16, 0x0000FFFF), (8, 0x00FF00FF), (4, 0x0F0F0F0F), (2, 0x33333333), (1, 0x55555555)):
        shift = jnp.full(a[0].shape, j, I32)
        for k in range(32):
            if k & j == 0:
                t = (lax.shift_right_logical(a[k], shift) ^ a[k + j]) & mask
                a[k + j] = a[k + j] ^ t
                a[k] = a[k] ^ lax.shift_left(t, shift)
    return a


def _attn_body(q_ref, iq_ref, iw_ref, k_ref, ik_ref, vt_ref, bias_ref, o_ref,
               iqt_ref, qt_ref, sc_ref, plane_ref, eq_ref, hit_ref, st_ref, acc_ref, m_ref, *, topk):
    i = pl.program_id(0)
    gq = A_GROUP * QB
    POS_BITS = (sc_ref.shape[0] - 1).bit_length()

    iqt = iq_ref[...].astype(F32).T
    iqt_ref[...] = jnp.zeros_like(iqt_ref)
    for h in range(IDX_HEADS):
        iqt_ref[0:IDX_DIM, h * QB:(h + 1) * QB] = iqt[h * IDX_DIM:(h + 1) * IDX_DIM, :].astype(BF16)
    qt = q_ref[...].astype(F32).T
    eye = (lax.broadcasted_iota(I32, (QB, QB), 0) == lax.broadcasted_iota(I32, (QB, QB), 1)).astype(BF16)
    for h in range(A_HEADS):
        qt_ref[0:A_HEAD_DIM, h * QB:(h + 1) * QB] = qt[h * A_HEAD_DIM:(h + 1) * A_HEAD_DIM, :].astype(BF16)
        qt_ref[A_HEAD_DIM:, h * QB:(h + 1) * QB] = eye
    iwt = iw_ref[...].T

    n_chunks = (i * QB + QB + SCORE_CHUNK - 1) // SCORE_CHUNK

    def score_chunk(c, carry):
        row0 = pl.multiple_of(c * SCORE_CHUNK, SCORE_CHUNK)
        ikc = ik_ref[pl.ds(row0, SCORE_CHUNK), :]
        sc = jnp.zeros((SCORE_CHUNK, QB), F32)
        for hp in range(IDX_HEADS // 2):
            raw = jnp.dot(ikc, iqt_ref[:, hp * 2 * QB:(hp + 1) * 2 * QB], preferred_element_type=F32)
            for u in range(2):
                h = 2 * hp + u
                sc = sc + jnp.maximum(raw[:, u * QB:(u + 1) * QB], 0.0) * iwt[IDX_DIM + h:IDX_DIM + h + 1, :]
        key_pos = row0 + lax.broadcasted_iota(I32, sc.shape, 0)
        q_pos = i * QB + lax.broadcasted_iota(I32, sc.shape, 1)
        bits = pltpu.bitcast(sc, I32)
        skey = jnp.where(bits >= 0, bits, bits ^ 0x7FFFFFFF)
        skey = jnp.where(key_pos <= q_pos, skey, INT_MIN)
        sc_ref[pl.ds(row0, SCORE_CHUNK), :] = skey
        ukey = skey ^ INT_MIN
        for gi in range(SCORE_CHUNK // PLANE_KEYS):
            base = gi * PLANE_KEYS
            planes = _bit_transpose32([ukey[base + 8 * j:base + 8 * (j + 1), :] for j in range(32)])
            for b in range(32):
                plane_ref[c * (SCORE_CHUNK // PLANE_KEYS) + gi, b] = planes[b]
        return carry

    lax.fori_loop(0, n_chunks, score_chunk, 0)

    eq_ref[...] = jnp.full_like(eq_ref, -1)
    hit_ref[...] = jnp.full_like(hit_ref, -1)
    groups_per_iter = SCORE_CHUNK // PLANE_KEYS

    def settle(g, keep_prev):
        hit = hit_ref[g]
        return jnp.where(keep_prev != 0, hit, eq_ref[g] ^ hit)

    def bit_pass(it, carry):
        chosen, n_greater, keep_prev = carry
        bit = 31 - it

        def sweep(gg, acc):
            for u in range(groups_per_iter):
                g = gg * groups_per_iter + u
                eq = settle(g, keep_prev)
                hit = eq & plane_ref[g, bit]
                eq_ref[g] = eq
                hit_ref[g] = hit
                acc = acc + lax.population_count(hit)
            return acc

        acc = lax.fori_loop(0, n_chunks, sweep, jnp.zeros((8, QB), I32))
        n_ge = n_greater + jnp.sum(acc, axis=0, keepdims=True)
        keep = n_ge >= topk
        chosen = jnp.where(keep, chosen | (jnp.int32(1) << bit), chosen)
        n_greater = jnp.where(keep, n_greater, n_ge)
        return chosen, n_greater, keep.astype(I32)

    zeros_row = jnp.zeros((1, QB), I32)
    chosen, n_greater, keep_last = lax.fori_loop(0, 32, bit_pass, (zeros_row, zeros_row, zeros_row + 1))
    thr = jnp.maximum(chosen ^ INT_MIN, INT_MIN + 1)

    def count_equal(gg, acc):
        for u in range(groups_per_iter):
            acc = acc + lax.population_count(settle(gg * groups_per_iter + u, keep_last))
        return acc

    n_equal = jnp.sum(lax.fori_loop(0, n_chunks, count_equal, jnp.zeros((8, QB), I32)), axis=0, keepdims=True)
    n_ge_thr = jnp.where(chosen == 0, 0, n_greater + n_equal)

    n_count = n_chunks * (SCORE_CHUNK // COUNT_ROWS)

    def count_keys(pred):
        def body(r, acc):
            row0 = pl.multiple_of(r * COUNT_ROWS, COUNT_ROWS)
            hit = jnp.where(pred(sc_ref[pl.ds(row0, COUNT_ROWS), :], row0), 1, 0).astype(I32)
            return acc + jnp.sum(hit.reshape(COUNT_ROWS // 8, 8, QB), axis=0)
        acc = lax.fori_loop(0, n_count, body, jnp.zeros((8, QB), I32))
        return jnp.sum(acc, axis=0, keepdims=True)

    @pl.when(jnp.max(n_ge_thr) > topk)
    def _():
        need = topk - n_greater

        def key_pos_of(blk, row0):
            return row0 + lax.broadcasted_iota(I32, blk.shape, 0)

        def pos_step(it, last):
            cand = last | (jnp.int32(1) << (POS_BITS - 1 - it))
            n_before = count_keys(lambda blk, row0: jnp.logical_and(blk == thr, key_pos_of(blk, row0) < cand))
            return jnp.where(n_before < need, cand, last)

        last = lax.fori_loop(0, POS_BITS, pos_step, jnp.zeros((1, QB), I32))

        def demote(r, carry):
            row0 = pl.multiple_of(r * COUNT_ROWS, COUNT_ROWS)
            blk = sc_ref[pl.ds(row0, COUNT_ROWS), :]
            late_tie = jnp.logical_and(blk == thr, key_pos_of(blk, row0) > last)
            sc_ref[pl.ds(row0, COUNT_ROWS), :] = jnp.where(late_tie, thr - 1, blk)
            return carry

        lax.fori_loop(0, n_count, demote, 0)

    m_ref[...] = jnp.full_like(m_ref, M_INIT)
    acc_ref[...] = jnp.zeros_like(acc_ref)
    heads = range(A_KV_HEADS)

    def logits(c, slot):
        row0 = pl.multiple_of(c * KB, KB)
        sel = sc_ref[pl.ds(row0, KB), :] >= thr
        mask_cols = jnp.where(sel, 0.0, MASKED_LOGIT).astype(BF16)
        kblk = k_ref[pl.ds(row0, KB), :]
        for j in heads:
            kaug = jnp.concatenate([kblk[:, j * A_HEAD_DIM:(j + 1) * A_HEAD_DIM], mask_cols], axis=1)
            st_ref[slot, j] = jnp.dot(kaug, qt_ref[:, j * gq:(j + 1) * gq], preferred_element_type=F32)

    def softmax_pv(c, slot, near):
        ps, alphas = [], []
        for j in heads:
            st = st_ref[slot, j]
            if near != (None, None):
                halves = [st[0:QB], st[QB:KB]]
                halves = [hv if n is None else hv + bias_ref[n, j] for hv, n in zip(halves, near)]
                st = jnp.concatenate(halves, axis=0)
            m_old = m_ref[j]
            m_new = jnp.maximum(m_old, jnp.max(st, axis=0, keepdims=True))
            m_ref[j] = m_new
            ps.append(jnp.exp2(st - m_new).astype(BF16))
            alphas.append(jnp.exp2(m_old - m_new))
        vblk = vt_ref[c]
        pvs = [jnp.dot(vblk[j * VT_ROWS:(j + 1) * VT_ROWS, :], ps[j], preferred_element_type=F32)
               for j in heads]
        for j in heads:
            acc_ref[j] = acc_ref[j] * alphas[j] + pvs[j]

    far = (None, None)
    n_far = jnp.maximum(lax.shift_right_arithmetic(i - 1, 1), 0)
    i_odd = (i & 1) == 1

    @pl.when(i >= 1)
    def _():
        logits(0, 0)

    @pl.when(i == 0)
    def _():
        logits(0, 1)

    def far_pair(pp, carry):
        c0 = 2 * pp
        logits(c0 + 1, 1)
        softmax_pv(c0, 0, far)
        logits(c0 + 2, 0)
        softmax_pv(c0 + 1, 1, far)
        return carry

    lax.fori_loop(0, lax.shift_right_arithmetic(n_far, 1), far_pair, 0)

    @pl.when((n_far & 1) == 1)
    def _():
        softmax_pv(n_far - 1, 0, far)
        logits(n_far, 0)

    @pl.when(i_odd)
    def _():
        softmax_pv(n_far, 0, (1, 0))

    @pl.when(jnp.logical_and(jnp.logical_not(i_odd), i >= 2))
    def _():
        logits(n_far + 1, 1)
        softmax_pv(n_far, 0, (None, 1))

    @pl.when(jnp.logical_not(i_odd))
    def _():
        softmax_pv(lax.shift_right_arithmetic(i, 1), 1, (0, None))

    for j in heads:
        acc = acc_ref[j]
        out_t = acc[0:A_HEAD_DIM, :] * (1.0 / acc[A_HEAD_DIM:A_HEAD_DIM + 1, :])
        for g in range(A_GROUP):
            h = j * A_GROUP + g
            o_ref[:, h * A_HEAD_DIM:(h + 1) * A_HEAD_DIM] = out_t[:, g * QB:(g + 1) * QB].T.astype(o_ref.dtype)


def _dsa_attention(main, iw, vt3, bias, topk):
    s = main.shape[0]
    nb = s // QB
    q_blk = A_Q_COLS // A_Q_COLS
    del q_blk
    return pl.pallas_call(
        functools.partial(_attn_body, topk=topk),
        grid=(nb,),
        in_specs=[
            pl.BlockSpec((QB, A_Q_COLS), lambda i: (i, 0)),
            pl.BlockSpec((QB, A_IQ_COLS), lambda i: (i, (A_Q_COLS + 2 * A_KV_COLS) // A_IQ_COLS)),
            pl.BlockSpec((QB, LANES), lambda i: (i, 0)),
            _resident((s, A_KV_COLS), lambda i: (0, A_Q_COLS // A_KV_COLS)),
            _resident((s, LANES), lambda i: (0, A_MAIN_COLS // LANES)),
            _resident((s // KB, A_KV_HEADS * VT_ROWS, KB), lambda i: (0, 0, 0)),
            _resident((2, A_KV_HEADS, QB, A_GROUP * QB), lambda i: (0, 0, 0, 0)),
        ],
        out_specs=pl.BlockSpec((QB, A_Q_COLS), lambda i: (i, 0)),
        out_shape=jax.ShapeDtypeStruct((s, A_Q_COLS), BF16),
        scratch_shapes=[
            pltpu.VMEM((LANES, IDX_HEADS * QB), BF16),
            pltpu.VMEM((A_HEAD_DIM + QB, A_HEADS * QB), BF16),
            pltpu.VMEM((s, QB), I32),
            pltpu.VMEM((s // PLANE_KEYS, 32, 8, QB), I32),
            pltpu.VMEM((s // PLANE_KEYS, 8, QB), I32),
            pltpu.VMEM((s // PLANE_KEYS, 8, QB), I32),
            pltpu.VMEM((2, A_KV_HEADS, KB, A_GROUP * QB), F32),
            pltpu.VMEM((A_KV_HEADS, VT_ROWS, A_GROUP * QB), F32),
            pltpu.VMEM((A_KV_HEADS, 1, A_GROUP * QB), F32),
        ],
        compiler_params=_params("arbitrary"),
        name="dsa_attention",
    )(main, main, iw, main, main, vt3, bias)


def _dsa_mixer(x, gain, w_cat, w_out, layer, q_gain, k_gain, ik_gain_padded, bias):
    s = x.shape[0]
    topk = min(TOPK_MAX, s // 4)
    main, iw = _dsa_in_proj(x, gain, w_cat, layer, q_gain, k_gain, ik_gain_padded)
    v = main[:, A_Q_COLS + A_KV_COLS:A_Q_COLS + 2 * A_KV_COLS]
    vt4 = v.reshape(s // KB, KB, A_KV_HEADS, A_HEAD_DIM).transpose(0, 2, 3, 1)
    tail = jnp.zeros((s // KB, A_KV_HEADS, VT_ROWS - A_HEAD_DIM, KB), BF16).at[:, :, 0, :].set(1.0)
    vt3 = jnp.concatenate([vt4, tail], axis=2).reshape(s // KB, A_KV_HEADS * VT_ROWS, KB)
    o = _dsa_attention(main, iw, vt3, bias, topk)
    return _out_proj(o, w_out, layer, x)


def _hgrn_sum_matrices():
    c = HG_CHUNK
    t = np.arange(c)[:, None]
    s = np.arange(c)[None, :]
    mats = []
    for l in range(1, HG_LEVELS + 1):
        blk = 1 << l
        m = (t // blk) * blk + blk // 2
        upper = t >= m
        mats.append(np.where(upper, (s >= m) & (s <= t), (s > t) & (s <= m - 1)))
    mats.append(s <= t)
    mats.append(s > t)
    return np.concatenate(mats, axis=0).astype(np.float32)


def _hgrn_split_levels():
    t = np.arange(HG_CHUNK)[:, None]
    s = np.arange(HG_CHUNK)[None, :]
    lvl = np.zeros((HG_CHUNK, HG_CHUNK), np.int32)
    for l in range(1, HG_LEVELS + 1):
        lvl[(s < t) & ((t >> l) == (s >> l)) & (((t >> (l - 1)) & 1) == 1) & (((s >> (l - 1)) & 1) == 0)] = l
    assert np.all((lvl > 0) == (s < t))
    return lvl


def _hgrn_body(q_ref, f_ref, i_ref, g_ref, lbl_ref, og_ref, gm_ref, split_ref, o_ref, st_ref, *, layer):
    c = pl.program_id(1)
    ch = HG_CHUNK

    @pl.when(c == 0)
    def _():
        st_ref[...] = jnp.zeros_like(st_ref)

    logits = lbl_ref[...]
    e = jnp.exp(logits - jnp.max(logits, axis=0, keepdims=True))
    p = e / jnp.sum(e, axis=0, keepdims=True)
    csum = p[0:1, :]
    for r in range(1, layer + 1):
        csum = csum + p[r:r + 1, :]
    lb = csum - p[0:1, :]

    def chunk(rows):
        fgate = lb + (1.0 - lb) * jax.nn.sigmoid(f_ref[rows, :])
        kin = 1.0 - fgate
        logf = jnp.log2(fgate)
        logf_hi = logf.astype(BF16)
        logf_lo = (logf - logf_hi.astype(F32)).astype(BF16)
        logf_split = jnp.concatenate([logf_hi, logf_lo], axis=0)
        decs = [jnp.dot(gm_ref[r * ch:(r + 1) * ch, :], logf_split, preferred_element_type=F32)
                for r in range(HG_LEVELS + 2)]

        row = lax.broadcasted_iota(I32, (ch, B_KEY_DIM), 0)
        split = split_ref[...]
        nt = (((1,), (1,)), ((), ()))
        heads = range(HG_HEADS_PER_STEP)
        levels = range(1, HG_LEVELS + 1)
        sls = [slice(hh * B_KEY_DIM, (hh + 1) * B_KEY_DIM) for hh in heads]

        qs = [q_ref[rows, sl] * (B_KEY_DIM ** -0.5) for sl in sls]
        ks = [kin[:, sl] for sl in sls]
        vs = [i_ref[rows, sl] for sl in sls]
        b_incl = [decs[HG_LEVELS][:, sl] for sl in sls]
        b_rest = [decs[HG_LEVELS + 1][:, sl] for sl in sls]
        states = [st_ref[hh] for hh in heads]

        zs = []
        for hh in heads:
            for l in levels:
                upper = (row & (1 << (l - 1))) != 0
                ex = jnp.exp2(decs[l - 1][:, sls[hh]])
                zs.append((jnp.where(upper, qs[hh], ks[hh]) * ex).astype(BF16))
        prods = [lax.dot_general(z, z, nt, preferred_element_type=F32) for z in zs]
        o_inter = [lax.dot_general((qs[hh] * jnp.exp2(b_incl[hh])).astype(BF16), states[hh].astype(BF16), nt,
                                   preferred_element_type=F32) for hh in heads]
        new_kv = [jnp.dot(vs[hh].T.astype(BF16), (ks[hh] * jnp.exp2(b_rest[hh])).astype(BF16),
                          preferred_element_type=F32) for hh in heads]
        a_mats = []
        for hh in heads:
            a = jnp.zeros((ch, ch), F32)
            for l in levels:
                a = jnp.where(split == l, prods[hh * HG_LEVELS + l - 1], a)
            a_mats.append(a.astype(BF16))
        o_intra = [jnp.dot(a_mats[hh], vs[hh].astype(BF16), preferred_element_type=F32) for hh in heads]

        for hh in heads:
            sl = sls[hh]
            st_ref[hh] = states[hh] * jnp.exp2(b_incl[hh][ch - 1:ch, :]) + new_kv[hh]
            o = o_inter[hh] + o_intra[hh] + jnp.sum(qs[hh] * ks[hh], axis=-1, keepdims=True) * vs[hh]
            on = o * _rms_scale(o) * og_ref[:, sl]
            gate = g_ref[rows, sl]
            o_ref[rows, sl] = (on * (gate * jax.nn.sigmoid(gate))).astype(o_ref.dtype)

    for cc in range(HG_CHUNKS_PER_STEP):
        chunk(slice(cc * ch, (cc + 1) * ch))


def _hgrn_core(proj, lb_logits, o_gain, layer):
    s = proj.shape[0]
    w = HG_HEADS_PER_STEP * B_KEY_DIM
    nhb = D_MODEL // w
    rows_per_step = HG_CHUNKS_PER_STEP * HG_CHUNK
    gm = jnp.asarray(np.tile(_hgrn_sum_matrices(), (1, 2)), dtype=BF16)
    split = jnp.asarray(_hgrn_split_levels())
    col = lambda off: (lambda hb, c: (c, off * nhb + hb))
    return pl.pallas_call(
        functools.partial(_hgrn_body, layer=layer),
        grid=(nhb, s // rows_per_step),
        in_specs=[
            pl.BlockSpec((rows_per_step, w), col(0)),
            pl.BlockSpec((rows_per_step, w), col(1)),
            pl.BlockSpec((rows_per_step, w), col(2)),
            pl.BlockSpec((rows_per_step, w), col(3)),
            pl.BlockSpec((DEPTH, w), lambda hb, c: (0, hb)),
            pl.BlockSpec((1, w), lambda hb, c: (0, hb)),
            pl.BlockSpec(gm.shape, lambda hb, c: (0, 0)),
            pl.BlockSpec(split.shape, lambda hb, c: (0, 0)),
        ],
        out_specs=pl.BlockSpec((rows_per_step, w), lambda hb, c: (c, hb)),
        out_shape=jax.ShapeDtypeStruct((s, D_MODEL), BF16),
        scratch_shapes=[pltpu.VMEM((HG_HEADS_PER_STEP, B_KEY_DIM, B_KEY_DIM), F32)],
        compiler_params=_params("parallel", "arbitrary"),
        name="hgrn_core",
    )(proj, proj, proj, proj, lb_logits, o_gain.reshape(1, D_MODEL), gm, split)


def _hgrn_mixer(x, gain, w_in, w_out, mixer_idx, lb_logits, o_gain, layer):
    proj = _hgrn_in_proj(x, gain, w_in, mixer_idx)
    o = _hgrn_core(proj, lb_logits, o_gain, layer)
    return _out_proj(o, w_out, mixer_idx, x)


def kernel(x, norm_gains, ffn_w_gate, ffn_w_up, ffn_w_down, dsa_w_in, dsa_w_out, dsa_q_gain, dsa_k_gain,
           dsa_idx_k_gain, rel_bias, hgrn_w_in, hgrn_w_out, hgrn_lb_logits, hgrn_o_gain):
    assert x.shape[0] == 1 and x.shape[2] == D_MODEL and x.shape[1] % FFN_TM == 0
    h = x.reshape(x.shape[1], D_MODEL)
    wg = ffn_w_gate.astype(BF16)
    wu = ffn_w_up.astype(BF16)
    wd = ffn_w_down.astype(BF16)
    pad_cols = NMM_TN - (A_IN_COLS - A_MAIN_COLS)
    dsa_w_cat = jnp.pad(dsa_w_in, ((0, 0), (0, 0), (0, pad_cols))).astype(BF16)
    dsa_wo = dsa_w_out.astype(BF16)
    ik_gain = jnp.pad(dsa_idx_k_gain, ((0, 0), (0, LANES - IDX_DIM)))
    hg_wi = hgrn_w_in.astype(BF16)
    hg_wo = hgrn_w_out.astype(BF16)
    bias = _near_bias(rel_bias)

    for layer in range(DEPTH):
        j = layer // N_MIXERS
        h = _ffn(h, norm_gains[layer, 0], wg, wu, wd, lead=(layer, 0))
        if layer % N_MIXERS == 0:
            h = _dsa_mixer(h, norm_gains[layer, 1], dsa_w_cat, dsa_wo, j, dsa_q_gain[j], dsa_k_gain[j],
                           ik_gain[j:j + 1], bias)
        else:
            h = _hgrn_mixer(h, norm_gains[layer, 1], hg_wi, hg_wo, j, hgrn_lb_logits, hgrn_o_gain[j], layer)
        h = _ffn(h, norm_gains[layer, 2], wg, wu, wd, lead=(layer, 1))
    return h.reshape(x.shape)
```

```python
import functools
import math

import numpy as np
import jax
import jax.numpy as jnp
from jax import lax
from jax.experimental import pallas as pl
from jax.experimental.pallas import tpu as pltpu

F32 = jnp.float32
BF16 = jnp.bfloat16
I32 = jnp.int32

D_MODEL = 2048
DEPTH = 4
N_MIXERS = 2
A_HEADS = 16
A_HEAD_DIM = 128
A_KV_HEADS = 4
A_GROUP = A_HEADS // A_KV_HEADS
IDX_HEADS = 16
IDX_DIM = 64
TOPK_MAX = 256
REL_BUCKETS = 32
REL_MAX_DIST = 128
B_KEY_DIM = 128
B_HEADS = D_MODEL // B_KEY_DIM
D_FF = 5632
EPS = 1e-6

A_Q_COLS = A_HEADS * A_HEAD_DIM
A_KV_COLS = A_KV_HEADS * A_HEAD_DIM
A_IQ_COLS = IDX_HEADS * IDX_DIM
A_MAIN_COLS = A_Q_COLS + 2 * A_KV_COLS + A_IQ_COLS
A_IN_COLS = A_MAIN_COLS + IDX_DIM + IDX_HEADS

LANES = 128
V7X_VMEM_BYTES = 64 * 1024 * 1024
VMEM_LIMIT_BYTES = 56 * 1024 * 1024

INT_MIN = -(2 ** 31)
LOG2_E = math.log2(math.e)
MASKED_LOGIT = -1e30
M_INIT = -1e20
VT_ROWS = A_HEAD_DIM + 16

FFN_TM = 1024
FFN_TF = 512
FFN_TF_FIRST = 256
FFN_COPY_ROWS = 128
NMM_TM = 1024
NMM_TN = 512
NMM_ROWS = 256
OUT_TM = 512
QB = 128
KB = 2 * QB
SCORE_CHUNK = 512
PLANE_KEYS = 256
COUNT_ROWS = 512
HG_CHUNK = 128
HG_HEADS_PER_STEP = 4
HG_CHUNKS_PER_STEP = 2
HG_LEVELS = 7


def _params(*semantics):
    return pltpu.CompilerParams(dimension_semantics=semantics, vmem_limit_bytes=VMEM_LIMIT_BYTES)


def _resident(shape, index_map):
    return pl.BlockSpec(shape, index_map, pipeline_mode=pl.Buffered(1))


def _rms_scale(x):
    return lax.rsqrt(jnp.mean(x * x, axis=-1, keepdims=True) + EPS)


def _ffn_step(j, n_steps, x_ref, g_ref, weights, o_ref, h_ref):
    @pl.when(j == 0)
    def _():
        x = x_ref[...]
        h_ref[...] = (x * _rms_scale(x) * g_ref[...]).astype(BF16)
        o_ref[...] = jnp.zeros_like(o_ref)

    h = h_ref[...]
    wg, wu, wd = weights()
    gate = jnp.dot(h, wg, preferred_element_type=F32)
    up = jnp.dot(h, wu, preferred_element_type=F32)
    act = (gate * jax.nn.sigmoid(gate) * up).astype(BF16)
    o_ref[...] += jnp.dot(act, wd, preferred_element_type=F32)

    @pl.when(j == n_steps - 1)
    def _():
        o_ref[...] = x_ref[...] + 0.5 * o_ref[...]


def _ffn_first_body(x_ref, g_ref, wg_ref, wu_ref, wd_ref, y_ref, wgb_ref, wub_ref, wdb_ref, h_ref):
    def weights():
        tiles = []
        for src, dst in ((wg_ref, wgb_ref), (wu_ref, wub_ref), (wd_ref, wdb_ref)):
            tiles.append(src[...].astype(BF16))
            dst[...] = tiles[-1]
        return tiles

    _ffn_step(pl.program_id(0), pl.num_programs(0), x_ref, g_ref, weights, y_ref, h_ref)


def _ffn_rest_body(x_ref, y0_ref, g_ref, wg_ref, wu_ref, wd_ref, o_ref, h_ref):
    i = pl.program_id(0)
    j = pl.program_id(1)

    @pl.when(jnp.logical_and(i == 0, j < o_ref.shape[0] // FFN_COPY_ROWS))
    def _():
        o_ref[pl.ds(pl.multiple_of(j * FFN_COPY_ROWS, FFN_COPY_ROWS), FFN_COPY_ROWS), :] = y0_ref[...]

    @pl.when(i > 0)
    def _():
        _ffn_step(j, pl.num_programs(1), x_ref, g_ref, lambda: (wg_ref[...], wu_ref[...], wd_ref[...]), o_ref, h_ref)


def _ffn(x, gain, wg32, wu32, wd32, lead):
    s, d = x.shape
    f = wg32.shape[-1]
    tm = min(FFN_TM, s)
    squeeze = (None,) * len(lead)
    n_copy = tm // FFN_COPY_ROWS
    assert f // FFN_TF >= n_copy
    y0, wg, wu, wd = pl.pallas_call(
        _ffn_first_body,
        grid=(f // FFN_TF_FIRST,),
        in_specs=[
            _resident((tm, d), lambda j: (0, 0)),
            pl.BlockSpec((1, d), lambda j: (0, 0)),
            pl.BlockSpec(squeeze + (d, FFN_TF_FIRST), lambda j: lead + (0, j)),
            pl.BlockSpec(squeeze + (d, FFN_TF_FIRST), lambda j: lead + (0, j)),
            pl.BlockSpec(squeeze + (FFN_TF_FIRST, d), lambda j: lead + (j, 0)),
        ],
        out_specs=(
            pl.BlockSpec((tm, d), lambda j: (0, 0)),
            pl.BlockSpec((d, FFN_TF_FIRST), lambda j: (0, j)),
            pl.BlockSpec((d, FFN_TF_FIRST), lambda j: (0, j)),
            pl.BlockSpec((FFN_TF_FIRST, d), lambda j: (j, 0)),
        ),
        out_shape=(jax.ShapeDtypeStruct((tm, d), F32), jax.ShapeDtypeStruct((d, f), BF16),
                   jax.ShapeDtypeStruct((d, f), BF16), jax.ShapeDtypeStruct((f, d), BF16)),
        scratch_shapes=[pltpu.VMEM((tm, d), BF16)],
        compiler_params=_params("arbitrary"),
        name="ffn_first",
    )(x, gain.reshape(1, d), wg32, wu32, wd32)
    wcol = lambda i, j: (0, jnp.where(i > 0, j, 0))
    wrow = lambda i, j: (jnp.where(i > 0, j, 0), 0)
    return pl.pallas_call(
        _ffn_rest_body,
        grid=(s // tm, f // FFN_TF),
        in_specs=[
            pl.BlockSpec((tm, d), lambda i, j: (i, 0)),
            pl.BlockSpec((FFN_COPY_ROWS, d), lambda i, j: (jnp.where(i == 0, jnp.minimum(j, n_copy - 1), n_copy - 1), 0),
                         pipeline_mode=pl.Buffered(1)),
            pl.BlockSpec((1, d), lambda i, j: (0, 0)),
            pl.BlockSpec((d, FFN_TF), wcol),
            pl.BlockSpec((d, FFN_TF), wcol),
            pl.BlockSpec((FFN_TF, d), wrow),
        ],
        out_specs=pl.BlockSpec((tm, d), lambda i, j: (i, 0)),
        out_shape=jax.ShapeDtypeStruct((s, d), F32),
        scratch_shapes=[pltpu.VMEM((tm, d), BF16)],
        compiler_params=_params("arbitrary", "arbitrary"),
        name="ffn_rest",
    )(x, y0, gain.reshape(1, d), wg, wu, wd)


def _nmm_body(*refs, n_extra, epilogue):
    x_ref, g_ref, w_ref = refs[:3]
    extra = refs[3:3 + n_extra]
    out_refs = refs[3 + n_extra:-1]
    h_ref = refs[-1]
    n = pl.program_id(1)

    @pl.when(n == 0)
    def _():
        x = x_ref[...]
        h_ref[...] = (x * _rms_scale(x) * g_ref[...]).astype(BF16)

    epilogue(n, h_ref, w_ref, extra, out_refs)


def _row_chunks(h_ref, w_ref, emit):
    for r in range(h_ref.shape[0] // NMM_ROWS):
        rows = slice(r * NMM_ROWS, (r + 1) * NMM_ROWS)
        emit(rows, jnp.dot(h_ref[rows, :], w_ref[...], preferred_element_type=F32))


def _norm_matmul(x, gain, w, layer, extra, extra_specs, out_shapes, out_specs, epilogue):
    s, d = x.shape
    ncols = w.shape[-1]
    tm = min(NMM_TM, s)
    return pl.pallas_call(
        functools.partial(_nmm_body, n_extra=len(extra), epilogue=epilogue),
        grid=(s // tm, ncols // NMM_TN),
        in_specs=[
            pl.BlockSpec((tm, d), lambda i, n: (i, 0)),
            pl.BlockSpec((1, d), lambda i, n: (0, 0)),
            pl.BlockSpec((None, d, NMM_TN), lambda i, n: (layer, 0, n)),
        ] + list(extra_specs),
        out_specs=out_specs,
        out_shape=out_shapes,
        scratch_shapes=[pltpu.VMEM((tm, d), BF16)],
        compiler_params=_params("parallel", "arbitrary"),
        name="norm_matmul",
    )(x, gain.reshape(1, d), w, *extra)


def _head_norm(y, gain, scale):
    parts = []
    for c in range(NMM_TN // A_HEAD_DIM):
        z = y[:, c * A_HEAD_DIM:(c + 1) * A_HEAD_DIM]
        parts.append(z * _rms_scale(z) * (gain * scale))
    return jnp.concatenate(parts, axis=1)


def _dsa_in_epilogue(n, h_ref, w_ref, extra, out_refs):
    qg_ref, kg_ref, ikg_ref = extra
    main_ref, iw_ref = out_refs
    q_tiles = A_Q_COLS // NMM_TN
    k_tile = q_tiles
    v_tile = k_tile + 1
    iq_tile0 = v_tile + 1
    idx_tile = A_MAIN_COLS // NMM_TN

    def store_main(fn):
        def emit(rows, y):
            main_ref[rows, :] = fn(y).astype(BF16)
        _row_chunks(h_ref, w_ref, emit)

    @pl.when(n < q_tiles)
    def _():
        store_main(lambda y: _head_norm(y, qg_ref[...], A_HEAD_DIM ** -0.5 * LOG2_E))

    @pl.when(n == k_tile)
    def _():
        store_main(lambda y: _head_norm(y, kg_ref[...], 1.0))

    @pl.when(n == v_tile)
    def _():
        store_main(lambda y: y)

    @pl.when(jnp.logical_and(n >= iq_tile0, n < idx_tile))
    def _():
        store_main(lambda y: y * IDX_DIM ** -0.5)

    @pl.when(n == idx_tile)
    def _():
        def emit(rows, y):
            z = y[:, :LANES]
            lane = lax.broadcasted_iota(I32, z.shape, 1)
            ms = jnp.sum(jnp.where(lane < IDX_DIM, z * z, 0.0), axis=-1, keepdims=True) * (1.0 / IDX_DIM)
            ikn = z * lax.rsqrt(ms + EPS) * ikg_ref[...]
            main_ref[rows, :] = jnp.concatenate(
                [ikn, jnp.zeros((y.shape[0], NMM_TN - LANES), F32)], axis=1).astype(BF16)
            iw_ref[rows, :] = z * IDX_HEADS ** -0.5
        _row_chunks(h_ref, w_ref, emit)


def _dsa_in_proj(x, gain, w_cat, layer, q_gain, k_gain, ik_gain_padded):
    s = x.shape[0]
    tm = min(NMM_TM, s)
    ncols = w_cat.shape[-1]
    vec = lambda i, n: (0, 0)
    return _norm_matmul(
        x, gain, w_cat, layer,
        extra=(q_gain.reshape(1, A_HEAD_DIM), k_gain.reshape(1, A_HEAD_DIM), ik_gain_padded),
        extra_specs=[pl.BlockSpec((1, A_HEAD_DIM), vec), pl.BlockSpec((1, A_HEAD_DIM), vec),
                     pl.BlockSpec((1, LANES), vec)],
        out_shapes=(jax.ShapeDtypeStruct((s, ncols), BF16), jax.ShapeDtypeStruct((s, LANES), F32)),
        out_specs=(pl.BlockSpec((tm, NMM_TN), lambda i, n: (i, n)),
                   pl.BlockSpec((tm, LANES), lambda i, n: (i, 0))),
        epilogue=_dsa_in_epilogue,
    )


def _identity_epilogue(n, h_ref, w_ref, extra, out_refs):
    def emit(rows, y):
        out_refs[0][rows, :] = y
    _row_chunks(h_ref, w_ref, emit)


def _hgrn_in_proj(x, gain, w, layer):
    s = x.shape[0]
    tm = min(NMM_TM, s)
    return _norm_matmul(
        x, gain, w, layer, extra=(), extra_specs=[],
        out_shapes=jax.ShapeDtypeStruct((s, w.shape[-1]), F32),
        out_specs=pl.BlockSpec((tm, NMM_TN), lambda i, n: (i, n)),
        epilogue=_identity_epilogue,
    )


def _out_proj_body(a_ref, w_ref, r_ref, o_ref):
    o_ref[...] = r_ref[...] + jnp.dot(a_ref[...], w_ref[...], preferred_element_type=F32)


def _out_proj(a, w, layer, res):
    s, k = a.shape
    d = w.shape[-1]
    tm = min(OUT_TM, s)
    return pl.pallas_call(
        _out_proj_body,
        grid=(s // tm,),
        in_specs=[
            pl.BlockSpec((tm, k), lambda i: (i, 0)),
            _resident((None, k, d), lambda i: (layer, 0, 0)),
            pl.BlockSpec((tm, d), lambda i: (i, 0)),
        ],
        out_specs=pl.BlockSpec((tm, d), lambda i: (i, 0)),
        out_shape=jax.ShapeDtypeStruct((s, d), F32),
        compiler_params=_params("parallel"),
        name="out_proj",
    )(a, w, res)


def _t5_bucket_table():
    exact = REL_BUCKETS // 2
    d = np.arange(2 * QB)
    df = np.maximum(d, 1).astype(np.float64)
    val = np.log(df / exact) / math.log(REL_MAX_DIST / exact) * (REL_BUCKETS - exact)
    big = d >= exact
    frac = np.abs(val[big] - np.round(val[big]))
    assert np.all((frac > 1e-4) | (frac == 0.0))
    large = np.minimum(exact + np.floor(val + 1e-9).astype(np.int64), REL_BUCKETS - 1)
    return np.where(d < exact, d, large).astype(np.int32)


def _near_bucket_tiles():
    table = _t5_bucket_table()
    r = np.arange(QB)[:, None]
    t = np.arange(QB)[None, :]
    tiles = [table[np.maximum(n * QB + t - r, 0)] for n in range(2)]
    return np.stack(tiles).astype(np.int32)


def _bias_body(rel_ref, bkt_ref, o_ref):
    j = pl.program_id(1)
    bkt = bkt_ref[0]
    for g in range(A_GROUP):
        h = j * A_GROUP + g
        far = rel_ref[REL_BUCKETS - 1, h]
        acc = jnp.zeros(bkt.shape, F32)
        for b in range(REL_BUCKETS):
            acc = jnp.where(bkt == b, (rel_ref[b, h] - far) * LOG2_E, acc)
        o_ref[0, 0, :, g * QB:(g + 1) * QB] = acc


def _near_bias(rel_bias):
    bkt = jnp.asarray(_near_bucket_tiles())
    return pl.pallas_call(
        _bias_body,
        grid=(2, A_KV_HEADS),
        in_specs=[
            pl.BlockSpec(memory_space=pltpu.SMEM),
            pl.BlockSpec((1, QB, QB), lambda n, j: (n, 0, 0)),
        ],
        out_specs=pl.BlockSpec((1, 1, QB, A_GROUP * QB), lambda n, j: (n, j, 0, 0)),
        out_shape=jax.ShapeDtypeStruct((2, A_KV_HEADS, QB, A_GROUP * QB), F32),
        compiler_params=_params("arbitrary", "arbitrary"),
        name="near_bias",
    )(rel_bias, bkt)


def _bit_transpose32(words):
    a = list(words)
    for j, mask in ((16, 0x0000FFFF), (8, 0x00FF00FF), (4, 0x0F0F0F0F), (2, 0x33333333), (1, 0x55555555)):
        shift = jnp.full(a[0].shape, j, I32)
        for k in range(32):
            if k & j == 0:
                t = (lax.shift_right_logical(a[k], shift) ^ a[k + j]) & mask
                a[k + j] = a[k + j] ^ t
                a[k] = a[k] ^ lax.shift_left(t, shift)
    return a


def _attn_body(q_ref, iq_ref, iw_ref, k_ref, ik_ref, vt_ref, bias_ref, o_ref,
               iqt_ref, qt_ref, sc_ref, plane_ref, eq_ref, hit_ref, st_ref, acc_ref, m_ref, *, topk):
    i = pl.program_id(0)
    gq = A_GROUP * QB
    POS_BITS = (sc_ref.shape[0] - 1).bit_length()

    iqt = iq_ref[...].astype(F32).T
    iqt_ref[...] = jnp.zeros_like(iqt_ref)
    for h in range(IDX_HEADS):
        iqt_ref[0:IDX_DIM, h * QB:(h + 1) * QB] = iqt[h * IDX_DIM:(h + 1) * IDX_DIM, :].astype(BF16)
    qt = q_ref[...].astype(F32).T
    eye = (lax.broadcasted_iota(I32, (QB, QB), 0) == lax.broadcasted_iota(I32, (QB, QB), 1)).astype(BF16)
    for h in range(A_HEADS):
        qt_ref[0:A_HEAD_DIM, h * QB:(h + 1) * QB] = qt[h * A_HEAD_DIM:(h + 1) * A_HEAD_DIM, :].astype(BF16)
        qt_ref[A_HEAD_DIM:, h * QB:(h + 1) * QB] = eye
    iwt = iw_ref[...].T

    n_chunks = (i * QB + QB + SCORE_CHUNK - 1) // SCORE_CHUNK

    def score_chunk(c, carry):
        row0 = pl.multiple_of(c * SCORE_CHUNK, SCORE_CHUNK)
        ikc = ik_ref[pl.ds(row0, SCORE_CHUNK), :]
        sc = jnp.zeros((SCORE_CHUNK, QB), F32)
        for hp in range(IDX_HEADS // 2):
            raw = jnp.dot(ikc, iqt_ref[:, hp * 2 * QB:(hp + 1) * 2 * QB], preferred_element_type=F32)
            for u in range(2):
                h = 2 * hp + u
                sc = sc + jnp.maximum(raw[:, u * QB:(u + 1) * QB], 0.0) * iwt[IDX_DIM + h:IDX_DIM + h + 1, :]
        key_pos = row0 + lax.broadcasted_iota(I32, sc.shape, 0)
        q_pos = i * QB + lax.broadcasted_iota(I32, sc.shape, 1)
        bits = pltpu.bitcast(sc, I32)
        skey = jnp.where(bits >= 0, bits, bits ^ 0x7FFFFFFF)
        skey = jnp.where(key_pos <= q_pos, skey, INT_MIN)
        sc_ref[pl.ds(row0, SCORE_CHUNK), :] = skey
        ukey = skey ^ INT_MIN
        for gi in range(SCORE_CHUNK // PLANE_KEYS):
            base = gi * PLANE_KEYS
            planes = _bit_transpose32([ukey[base + 8 * j:base + 8 * (j + 1), :] for j in range(32)])
            for b in range(32):
                plane_ref[c * (SCORE_CHUNK // PLANE_KEYS) + gi, b] = planes[b]
        return carry

    lax.fori_loop(0, n_chunks, score_chunk, 0)

    eq_ref[...] = jnp.full_like(eq_ref, -1)
    hit_ref[...] = jnp.full_like(hit_ref, -1)
    groups_per_iter = SCORE_CHUNK // PLANE_KEYS

    def settle(g, keep_prev):
        hit = hit_ref[g]
        return jnp.where(keep_prev != 0, hit, eq_ref[g] ^ hit)

    def bit_pass(it, carry):
        chosen, n_greater, keep_prev = carry
        bit = 31 - it

        def sweep(gg, acc):
            for u in range(groups_per_iter):
                g = gg * groups_per_iter + u
                eq = settle(g, keep_prev)
                hit = eq & plane_ref[g, bit]
                eq_ref[g] = eq
                hit_ref[g] = hit
                acc = acc + lax.population_count(hit)
            return acc

        acc = lax.fori_loop(0, n_chunks, sweep, jnp.zeros((8, QB), I32))
        n_ge = n_greater + jnp.sum(acc, axis=0, keepdims=True)
        keep = n_ge >= topk
        chosen = jnp.where(keep, chosen | (jnp.int32(1) << bit), chosen)
        n_greater = jnp.where(keep, n_greater, n_ge)
        return chosen, n_greater, keep.astype(I32)

    zeros_row = jnp.zeros((1, QB), I32)
    chosen, n_greater, keep_last = lax.fori_loop(0, 32, bit_pass, (zeros_row, zeros_row, zeros_row + 1))
    thr = jnp.maximum(chosen ^ INT_MIN, INT_MIN + 1)

    def count_equal(gg, acc):
        for u in range(groups_per_iter):
            acc = acc + lax.population_count(settle(gg * groups_per_iter + u, keep_last))
        return acc

    n_equal = jnp.sum(lax.fori_loop(0, n_chunks, count_equal, jnp.zeros((8, QB), I32)), axis=0, keepdims=True)
    n_ge_thr = jnp.where(chosen == 0, 0, n_greater + n_equal)

    n_count = n_chunks * (SCORE_CHUNK // COUNT_ROWS)

    def count_keys(pred):
        def body(r, acc):
            row0 = pl.multiple_of(r * COUNT_ROWS, COUNT_ROWS)
            hit = jnp.where(pred(sc_ref[pl.ds(row0, COUNT_ROWS), :], row0), 1, 0).astype(I32)
            return acc + jnp.sum(hit.reshape(COUNT_ROWS // 8, 8, QB), axis=0)
        acc = lax.fori_loop(0, n_count, body, jnp.zeros((8, QB), I32))
        return jnp.sum(acc, axis=0, keepdims=True)

    @pl.when(jnp.max(n_ge_thr) > topk)
    def _():
        need = topk - n_greater

        def key_pos_of(blk, row0):
            return row0 + lax.broadcasted_iota(I32, blk.shape, 0)

        def pos_step(it, last):
            cand = last | (jnp.int32(1) << (POS_BITS - 1 - it))
            n_before = count_keys(lambda blk, row0: jnp.logical_and(blk == thr, key_pos_of(blk, row0) < cand))
            return jnp.where(n_before < need, cand, last)

        last = lax.fori_loop(0, POS_BITS, pos_step, jnp.zeros((1, QB), I32))

        def demote(r, carry):
            row0 = pl.multiple_of(r * COUNT_ROWS, COUNT_ROWS)
            blk = sc_ref[pl.ds(row0, COUNT_ROWS), :]
            late_tie = jnp.logical_and(blk == thr, key_pos_of(blk, row0) > last)
            sc_ref[pl.ds(row0, COUNT_ROWS), :] = jnp.where(late_tie, thr - 1, blk)
            return carry

        lax.fori_loop(0, n_count, demote, 0)

    m_ref[...] = jnp.full_like(m_ref, M_INIT)
    acc_ref[...] = jnp.zeros_like(acc_ref)
    heads = range(A_KV_HEADS)

    def logits(c, slot):
        row0 = pl.multiple_of(c * KB, KB)
        sel = sc_ref[pl.ds(row0, KB), :] >= thr
        mask_cols = jnp.where(sel, 0.0, MASKED_LOGIT).astype(BF16)
        kblk = k_ref[pl.ds(row0, KB), :]
        for j in heads:
            kaug = jnp.concatenate([kblk[:, j * A_HEAD_DIM:(j + 1) * A_HEAD_DIM], mask_cols], axis=1)
            st_ref[slot, j] = jnp.dot(kaug, qt_ref[:, j * gq:(j + 1) * gq], preferred_element_type=F32)

    def softmax_pv(c, slot, near):
        ps, alphas = [], []
        for j in heads:
            st = st_ref[slot, j]
            if near != (None, None):
                halves = [st[0:QB], st[QB:KB]]
                halves = [hv if n is None else hv + bias_ref[n, j] for hv, n in zip(halves, near)]
                st = jnp.concatenate(halves, axis=0)
            m_old = m_ref[j]
            m_new = jnp.maximum(m_old, jnp.max(st, axis=0, keepdims=True))
            m_ref[j] = m_new
            ps.append(jnp.exp2(st - m_new).astype(BF16))
            alphas.append(jnp.exp2(m_old - m_new))
        vblk = vt_ref[c]
        pvs = [jnp.dot(vblk[j * VT_ROWS:(j + 1) * VT_ROWS, :], ps[j], preferred_element_type=F32)
               for j in heads]
        for j in heads:
            acc_ref[j] = acc_ref[j] * alphas[j] + pvs[j]

    far = (None, None)
    n_far = jnp.maximum(lax.shift_right_arithmetic(i - 1, 1), 0)
    i_odd = (i & 1) == 1

    @pl.when(i >= 1)
    def _():
        logits(0, 0)

    @pl.when(i == 0)
    def _():
        logits(0, 1)

    def far_pair(pp, carry):
        c0 = 2 * pp
        logits(c0 + 1, 1)
        softmax_pv(c0, 0, far)
        logits(c0 + 2, 0)
        softmax_pv(c0 + 1, 1, far)
        return carry

    lax.fori_loop(0, lax.shift_right_arithmetic(n_far, 1), far_pair, 0)

    @pl.when((n_far & 1) == 1)
    def _():
        softmax_pv(n_far - 1, 0, far)
        logits(n_far, 0)

    @pl.when(i_odd)
    def _():
        softmax_pv(n_far, 0, (1, 0))

    @pl.when(jnp.logical_and(jnp.logical_not(i_odd), i >= 2))
    def _():
        logits(n_far + 1, 1)
        softmax_pv(n_far, 0, (None, 1))

    @pl.when(jnp.logical_not(i_odd))
    def _():
        softmax_pv(lax.shift_right_arithmetic(i, 1), 1, (0, None))

    for j in heads:
        acc = acc_ref[j]
        out_t = acc[0:A_HEAD_DIM, :] * (1.0 / acc[A_HEAD_DIM:A_HEAD_DIM + 1, :])
        for g in range(A_GROUP):
            h = j * A_GROUP + g
            o_ref[:, h * A_HEAD_DIM:(h + 1) * A_HEAD_DIM] = out_t[:, g * QB:(g + 1) * QB].T.astype(o_ref.dtype)


def _dsa_attention(main, iw, vt3, bias, topk):
    s = main.shape[0]
    nb = s // QB
    q_blk = A_Q_COLS // A_Q_COLS
    del q_blk
    return pl.pallas_call(
        functools.partial(_attn_body, topk=topk),
        grid=(nb,),
        in_specs=[
            pl.BlockSpec((QB, A_Q_COLS), lambda i: (i, 0)),
            pl.BlockSpec((QB, A_IQ_COLS), lambda i: (i, (A_Q_COLS + 2 * A_KV_COLS) // A_IQ_COLS)),
            pl.BlockSpec((QB, LANES), lambda i: (i, 0)),
            _resident((s, A_KV_COLS), lambda i: (0, A_Q_COLS // A_KV_COLS)),
            _resident((s, LANES), lambda i: (0, A_MAIN_COLS // LANES)),
            _resident((s // KB, A_KV_HEADS * VT_ROWS, KB), lambda i: (0, 0, 0)),
            _resident((2, A_KV_HEADS, QB, A_GROUP * QB), lambda i: (0, 0, 0, 0)),
        ],
        out_specs=pl.BlockSpec((QB, A_Q_COLS), lambda i: (i, 0)),
        out_shape=jax.ShapeDtypeStruct((s, A_Q_COLS), BF16),
        scratch_shapes=[
            pltpu.VMEM((LANES, IDX_HEADS * QB), BF16),
            pltpu.VMEM((A_HEAD_DIM + QB, A_HEADS * QB), BF16),
            pltpu.VMEM((s, QB), I32),
            pltpu.VMEM((s // PLANE_KEYS, 32, 8, QB), I32),
            pltpu.VMEM((s // PLANE_KEYS, 8, QB), I32),
            pltpu.VMEM((s // PLANE_KEYS, 8, QB), I32),
            pltpu.VMEM((2, A_KV_HEADS, KB, A_GROUP * QB), F32),
            pltpu.VMEM((A_KV_HEADS, VT_ROWS, A_GROUP * QB), F32),
            pltpu.VMEM((A_KV_HEADS, 1, A_GROUP * QB), F32),
        ],
        compiler_params=_params("arbitrary"),
        name="dsa_attention",
    )(main, main, iw, main, main, vt3, bias)


def _dsa_mixer(x, gain, w_cat, w_out, layer, q_gain, k_gain, ik_gain_padded, bias):
    s = x.shape[0]
    topk = min(TOPK_MAX, s // 4)
    main, iw = _dsa_in_proj(x, gain, w_cat, layer, q_gain, k_gain, ik_gain_padded)
    v = main[:, A_Q_COLS + A_KV_COLS:A_Q_COLS + 2 * A_KV_COLS]
    vt4 = v.reshape(s // KB, KB, A_KV_HEADS, A_HEAD_DIM).transpose(0, 2, 3, 1)
    tail = jnp.zeros((s // KB, A_KV_HEADS, VT_ROWS - A_HEAD_DIM, KB), BF16).at[:, :, 0, :].set(1.0)
    vt3 = jnp.concatenate([vt4, tail], axis=2).reshape(s // KB, A_KV_HEADS * VT_ROWS, KB)
    o = _dsa_attention(main, iw, vt3, bias, topk)
    return _out_proj(o, w_out, layer, x)


def _hgrn_sum_matrices():
    c = HG_CHUNK
    t = np.arange(c)[:, None]
    s = np.arange(c)[None, :]
    mats = []
    for l in range(1, HG_LEVELS + 1):
        blk = 1 << l
        m = (t // blk) * blk + blk // 2
        upper = t >= m
        mats.append(np.where(upper, (s >= m) & (s <= t), (s > t) & (s <= m - 1)))
    mats.append(s <= t)
    mats.append(s > t)
    return np.concatenate(mats, axis=0).astype(np.float32)


def _hgrn_split_levels():
    t = np.arange(HG_CHUNK)[:, None]
    s = np.arange(HG_CHUNK)[None, :]
    lvl = np.zeros((HG_CHUNK, HG_CHUNK), np.int32)
    for l in range(1, HG_LEVELS + 1):
        lvl[(s < t) & ((t >> l) == (s >> l)) & (((t >> (l - 1)) & 1) == 1) & (((s >> (l - 1)) & 1) == 0)] = l
    assert np.all((lvl > 0) == (s < t))
    return lvl


def _hgrn_body(q_ref, f_ref, i_ref, g_ref, lbl_ref, og_ref, gm_ref, split_ref, o_ref, st_ref, *, layer):
    c = pl.program_id(1)
    ch = HG_CHUNK

    @pl.when(c == 0)
    def _():
        st_ref[...] = jnp.zeros_like(st_ref)

    logits = lbl_ref[...]
    e = jnp.exp(logits - jnp.max(logits, axis=0, keepdims=True))
    p = e / jnp.sum(e, axis=0, keepdims=True)
    csum = p[0:1, :]
    for r in range(1, layer + 1):
        csum = csum + p[r:r + 1, :]
    lb = csum - p[0:1, :]

    def chunk(rows):
        fgate = lb + (1.0 - lb) * jax.nn.sigmoid(f_ref[rows, :])
        kin = 1.0 - fgate
        logf = jnp.log2(fgate)
        logf_hi = logf.astype(BF16)
        logf_lo = (logf - logf_hi.astype(F32)).astype(BF16)
        logf_split = jnp.concatenate([logf_hi, logf_lo], axis=0)
        decs = [jnp.dot(gm_ref[r * ch:(r + 1) * ch, :], logf_split, preferred_element_type=F32)
                for r in range(HG_LEVELS + 2)]

        row = lax.broadcasted_iota(I32, (ch, B_KEY_DIM), 0)
        split = split_ref[...]
        nt = (((1,), (1,)), ((), ()))
        heads = range(HG_HEADS_PER_STEP)
        levels = range(1, HG_LEVELS + 1)
        sls = [slice(hh * B_KEY_DIM, (hh + 1) * B_KEY_DIM) for hh in heads]

        qs = [q_ref[rows, sl] * (B_KEY_DIM ** -0.5) for sl in sls]
        ks = [kin[:, sl] for sl in sls]
        vs = [i_ref[rows, sl] for sl in sls]
        b_incl = [decs[HG_LEVELS][:, sl] for sl in sls]
        b_rest = [decs[HG_LEVELS + 1][:, sl] for sl in sls]
        states = [st_ref[hh] for hh in heads]

        zs = []
        for hh in heads:
            for l in levels:
                upper = (row & (1 << (l - 1))) != 0
                ex = jnp.exp2(decs[l - 1][:, sls[hh]])
                zs.append((jnp.where(upper, qs[hh], ks[hh]) * ex).astype(BF16))
        prods = [lax.dot_general(z, z, nt, preferred_element_type=F32) for z in zs]
        o_inter = [lax.dot_general((qs[hh] * jnp.exp2(b_incl[hh])).astype(BF16), states[hh].astype(BF16), nt,
                                   preferred_element_type=F32) for hh in heads]
        new_kv = [jnp.dot(vs[hh].T.astype(BF16), (ks[hh] * jnp.exp2(b_rest[hh])).astype(BF16),
                          preferred_element_type=F32) for hh in heads]
        a_mats = []
        for hh in heads:
            a = jnp.zeros((ch, ch), F32)
            for l in levels:
                a = jnp.where(split == l, prods[hh * HG_LEVELS + l - 1], a)
            a_mats.append(a.astype(BF16))
        o_intra = [jnp.dot(a_mats[hh], vs[hh].astype(BF16), preferred_element_type=F32) for hh in heads]

        for hh in heads:
            sl = sls[hh]
            st_ref[hh] = states[hh] * jnp.exp2(b_incl[hh][ch - 1:ch, :]) + new_kv[hh]
            o = o_inter[hh] + o_intra[hh] + jnp.sum(qs[hh] * ks[hh], axis=-1, keepdims=True) * vs[hh]
            on = o * _rms_scale(o) * og_ref[:, sl]
            gate = g_ref[rows, sl]
            o_ref[rows, sl] = (on * (gate * jax.nn.sigmoid(gate))).astype(o_ref.dtype)

    for cc in range(HG_CHUNKS_PER_STEP):
        chunk(slice(cc * ch, (cc + 1) * ch))


def _hgrn_core(proj, lb_logits, o_gain, layer):
    s = proj.shape[0]
    w = HG_HEADS_PER_STEP * B_KEY_DIM
    nhb = D_MODEL // w
    rows_per_step = HG_CHUNKS_PER_STEP * HG_CHUNK
    gm = jnp.asarray(np.tile(_hgrn_sum_matrices(), (1, 2)), dtype=BF16)
    split = jnp.asarray(_hgrn_split_levels())
    col = lambda off: (lambda hb, c: (c, off * nhb + hb))
    return pl.pallas_call(
        functools.partial(_hgrn_body, layer=layer),
        grid=(nhb, s // rows_per_step),
        in_specs=[
            pl.BlockSpec((rows_per_step, w), col(0)),
            pl.BlockSpec((rows_per_step, w), col(1)),
            pl.BlockSpec((rows_per_step, w), col(2)),
            pl.BlockSpec((rows_per_step, w), col(3)),
            pl.BlockSpec((DEPTH, w), lambda hb, c: (0, hb)),
            pl.BlockSpec((1, w), lambda hb, c: (0, hb)),
            pl.BlockSpec(gm.shape, lambda hb, c: (0, 0)),
            pl.BlockSpec(split.shape, lambda hb, c: (0, 0)),
        ],
        out_specs=pl.BlockSpec((rows_per_step, w), lambda hb, c: (c, hb)),
        out_shape=jax.ShapeDtypeStruct((s, D_MODEL), BF16),
        scratch_shapes=[pltpu.VMEM((HG_HEADS_PER_STEP, B_KEY_DIM, B_KEY_DIM), F32)],
        compiler_params=_params("parallel", "arbitrary"),
        name="hgrn_core",
    )(proj, proj, proj, proj, lb_logits, o_gain.reshape(1, D_MODEL), gm, split)


def _hgrn_mixer(x, gain, w_in, w_out, mixer_idx, lb_logits, o_gain, layer):
    proj = _hgrn_in_proj(x, gain, w_in, mixer_idx)
    o = _hgrn_core(proj, lb_logits, o_gain, layer)
    return _out_proj(o, w_out, mixer_idx, x)


def kernel(x, norm_gains, ffn_w_gate, ffn_w_up, ffn_w_down, dsa_w_in, dsa_w_out, dsa_q_gain, dsa_k_gain,
           dsa_idx_k_gain, rel_bias, hgrn_w_in, hgrn_w_out, hgrn_lb_logits, hgrn_o_gain):
    assert x.shape[0] == 1 and x.shape[2] == D_MODEL and x.shape[1] % FFN_TM == 0
    h = x.reshape(x.shape[1], D_MODEL)
    pad_cols = NMM_TN - (A_IN_COLS - A_MAIN_COLS)
    dsa_w_cat = jnp.pad(dsa_w_in, ((0, 0), (0, 0), (0, pad_cols))).astype(BF16)
    dsa_wo = dsa_w_out.astype(BF16)
    ik_gain = jnp.pad(dsa_idx_k_gain, ((0, 0), (0, LANES - IDX_DIM)))
    hg_wi = hgrn_w_in.astype(BF16)
    hg_wo = hgrn_w_out.astype(BF16)
    bias = _near_bias(rel_bias)

    for layer in range(DEPTH):
        j = layer // N_MIXERS
        h = _ffn(h, norm_gains[layer, 0], ffn_w_gate, ffn_w_up, ffn_w_down, (layer, 0))
        if layer % N_MIXERS == 0:
            h = _dsa_mixer(h, norm_gains[layer, 1], dsa_w_cat, dsa_wo, j, dsa_q_gain[j], dsa_k_gain[j],
                           ik_gain[j:j + 1], bias)
        else:
            h = _hgrn_mixer(h, norm_gains[layer, 1], hg_wi, hg_wo, j, hgrn_lb_logits, hgrn_o_gain[j], layer)
        h = _ffn(h, norm_gains[layer, 2], ffn_w_gate, ffn_w_up, ffn_w_down, (layer, 1))
    return h.reshape(x.shape)
```

```python
import functools
import math

import numpy as np
import jax
import jax.numpy as jnp
from jax import lax
from jax.experimental import pallas as pl
from jax.experimental.pallas import tpu as pltpu

F32 = jnp.float32
BF16 = jnp.bfloat16
I32 = jnp.int32

D_MODEL = 2048
DEPTH = 4
N_MIXERS = 2
A_HEADS = 16
A_HEAD_DIM = 128
A_KV_HEADS = 4
A_GROUP = A_HEADS // A_KV_HEADS
IDX_HEADS = 16
IDX_DIM = 64
TOPK_MAX = 256
REL_BUCKETS = 32
REL_MAX_DIST = 128
B_KEY_DIM = 128
B_HEADS = D_MODEL // B_KEY_DIM
D_FF = 5632
EPS = 1e-6

A_Q_COLS = A_HEADS * A_HEAD_DIM
A_KV_COLS = A_KV_HEADS * A_HEAD_DIM
A_IQ_COLS = IDX_HEADS * IDX_DIM
A_MAIN_COLS = A_Q_COLS + 2 * A_KV_COLS + A_IQ_COLS
A_IN_COLS = A_MAIN_COLS + IDX_DIM + IDX_HEADS

LANES = 128
V7X_VMEM_BYTES = 64 * 1024 * 1024
VMEM_LIMIT_BYTES = 56 * 1024 * 1024

INT_MIN = -(2 ** 31)
LOG2_E = math.log2(math.e)
MASKED_LOGIT = -1e30
M_INIT = -1e20
VT_ROWS = A_HEAD_DIM + 16

FFN_TM = 1024
FFN_TF = 512
FFN_TF_FIRST = 256
FFN_COPY_ROWS = 128
NMM_TM = 1024
NMM_TN = 512
HG_IN_TN = 1024
NMM_ROWS = 256
OUT_TM = 512
QB = 128
KB = 2 * QB
SCORE_CHUNK = 512
PLANE_KEYS = 256
COUNT_ROWS = 512
HG_CHUNK = 128
HG_HEADS_PER_STEP = 4
HG_CHUNKS_PER_STEP = 4
HG_LEVELS = 7


def _params(*semantics):
    return pltpu.CompilerParams(dimension_semantics=semantics, vmem_limit_bytes=VMEM_LIMIT_BYTES)


def _resident(shape, index_map):
    return pl.BlockSpec(shape, index_map, pipeline_mode=pl.Buffered(1))


def _rms_scale(x):
    return lax.rsqrt(jnp.mean(x * x, axis=-1, keepdims=True) + EPS)


def _ffn_step(j, n_steps, x_ref, g_ref, weights, o_ref, h_ref):
    @pl.when(j == 0)
    def _():
        x = x_ref[...]
        h_ref[...] = (x * _rms_scale(x) * g_ref[...]).astype(BF16)
        o_ref[...] = jnp.zeros_like(o_ref)

    h = h_ref[...]
    wg, wu, wd = weights()
    gate = jnp.dot(h, wg, preferred_element_type=F32)
    up = jnp.dot(h, wu, preferred_element_type=F32)
    act = (gate * jax.nn.sigmoid(gate) * up).astype(BF16)
    o_ref[...] += jnp.dot(act, wd, preferred_element_type=F32)

    @pl.when(j == n_steps - 1)
    def _():
        o_ref[...] = x_ref[...] + 0.5 * o_ref[...]


def _ffn_first_body(x_ref, g_ref, wg_ref, wu_ref, wd_ref, y_ref, wgb_ref, wub_ref, wdb_ref, h_ref):
    def weights():
        tiles = []
        for src, dst in ((wg_ref, wgb_ref), (wu_ref, wub_ref), (wd_ref, wdb_ref)):
            tiles.append(src[...].astype(BF16))
            dst[...] = tiles[-1]
        return tiles

    _ffn_step(pl.program_id(0), pl.num_programs(0), x_ref, g_ref, weights, y_ref, h_ref)


def _ffn_rest_body(x_ref, y0_ref, g_ref, wg_ref, wu_ref, wd_ref, o_ref, h_ref):
    i = pl.program_id(0)
    j = pl.program_id(1)

    @pl.when(jnp.logical_and(i == 0, j < o_ref.shape[0] // FFN_COPY_ROWS))
    def _():
        o_ref[pl.ds(pl.multiple_of(j * FFN_COPY_ROWS, FFN_COPY_ROWS), FFN_COPY_ROWS), :] = y0_ref[...]

    @pl.when(i > 0)
    def _():
        _ffn_step(j, pl.num_programs(1), x_ref, g_ref, lambda: (wg_ref[...], wu_ref[...], wd_ref[...]), o_ref, h_ref)


def _ffn(x, gain, wg32, wu32, wd32, lead):
    s, d = x.shape
    f = wg32.shape[-1]
    tm = min(FFN_TM, s)
    squeeze = (None,) * len(lead)
    n_copy = tm // FFN_COPY_ROWS
    assert f // FFN_TF >= n_copy
    y0, wg, wu, wd = pl.pallas_call(
        _ffn_first_body,
        grid=(f // FFN_TF_FIRST,),
        in_specs=[
            _resident((tm, d), lambda j: (0, 0)),
            pl.BlockSpec((1, d), lambda j: (0, 0)),
            pl.BlockSpec(squeeze + (d, FFN_TF_FIRST), lambda j: lead + (0, j)),
            pl.BlockSpec(squeeze + (d, FFN_TF_FIRST), lambda j: lead + (0, j)),
            pl.BlockSpec(squeeze + (FFN_TF_FIRST, d), lambda j: lead + (j, 0)),
        ],
        out_specs=(
            pl.BlockSpec((tm, d), lambda j: (0, 0)),
            pl.BlockSpec((d, FFN_TF_FIRST), lambda j: (0, j)),
            pl.BlockSpec((d, FFN_TF_FIRST), lambda j: (0, j)),
            pl.BlockSpec((FFN_TF_FIRST, d), lambda j: (j, 0)),
        ),
        out_shape=(jax.ShapeDtypeStruct((tm, d), F32), jax.ShapeDtypeStruct((d, f), BF16),
                   jax.ShapeDtypeStruct((d, f), BF16), jax.ShapeDtypeStruct((f, d), BF16)),
        scratch_shapes=[pltpu.VMEM((tm, d), BF16)],
        compiler_params=_params("arbitrary"),
        name="ffn_first",
    )(x, gain.reshape(1, d), wg32, wu32, wd32)
    wcol = lambda i, j: (0, jnp.where(i > 0, j, 0))
    wrow = lambda i, j: (jnp.where(i > 0, j, 0), 0)
    return pl.pallas_call(
        _ffn_rest_body,
        grid=(s // tm, f // FFN_TF),
        in_specs=[
            pl.BlockSpec((tm, d), lambda i, j: (jnp.minimum(jnp.maximum(i, 1), s // tm - 1), 0)),
            pl.BlockSpec((FFN_COPY_ROWS, d), lambda i, j: (jnp.where(i == 0, jnp.minimum(j, n_copy - 1), n_copy - 1), 0),
                         pipeline_mode=pl.Buffered(1)),
            pl.BlockSpec((1, d), lambda i, j: (0, 0)),
            pl.BlockSpec((d, FFN_TF), wcol),
            pl.BlockSpec((d, FFN_TF), wcol),
            pl.BlockSpec((FFN_TF, d), wrow),
        ],
        out_specs=pl.BlockSpec((tm, d), lambda i, j: (i, 0)),
        out_shape=jax.ShapeDtypeStruct((s, d), F32),
        scratch_shapes=[pltpu.VMEM((tm, d), BF16)],
        compiler_params=_params("arbitrary", "arbitrary"),
        name="ffn_rest",
    )(x, y0, gain.reshape(1, d), wg, wu, wd)


def _nmm_body(*refs, n_extra, epilogue):
    x_ref, g_ref, w_ref = refs[:3]
    extra = refs[3:3 + n_extra]
    out_refs = refs[3 + n_extra:-1]
    h_ref = refs[-1]
    n = pl.program_id(1)

    @pl.when(n == 0)
    def _():
        x = x_ref[...]
        h_ref[...] = (x * _rms_scale(x) * g_ref[...]).astype(BF16)

    epilogue(n, h_ref, w_ref, extra, out_refs)


def _row_chunks(h_ref, w_ref, emit):
    w = w_ref[...].astype(BF16)
    for r in range(h_ref.shape[0] // NMM_ROWS):
        rows = slice(r * NMM_ROWS, (r + 1) * NMM_ROWS)
        emit(rows, jnp.dot(h_ref[rows, :], w, preferred_element_type=F32))


def _norm_matmul(x, gain, w, layer, n_tiles, extra, extra_specs, out_shapes, out_specs, epilogue, tn=NMM_TN):
    s, d = x.shape
    tm = min(NMM_TM, s)
    last_whole = w.shape[-1] // tn - 1
    return pl.pallas_call(
        functools.partial(_nmm_body, n_extra=len(extra), epilogue=epilogue),
        grid=(s // tm, n_tiles),
        in_specs=[
            pl.BlockSpec((tm, d), lambda i, n: (i, 0)),
            pl.BlockSpec((1, d), lambda i, n: (0, 0)),
            pl.BlockSpec((None, d, tn), lambda i, n: (layer, 0, jnp.minimum(n, last_whole))),
        ] + list(extra_specs),
        out_specs=out_specs,
        out_shape=out_shapes,
        scratch_shapes=[pltpu.VMEM((tm, d), BF16)],
        compiler_params=_params("parallel", "arbitrary"),
        name="norm_matmul",
    )(x, gain.reshape(1, d), w, *extra)


def _head_norm(y, gain, scale):
    parts = []
    for c in range(NMM_TN // A_HEAD_DIM):
        z = y[:, c * A_HEAD_DIM:(c + 1) * A_HEAD_DIM]
        parts.append(z * _rms_scale(z) * (gain * scale))
    return jnp.concatenate(parts, axis=1)


def _dsa_in_epilogue(n, h_ref, w_ref, extra, out_refs):
    qg_ref, kg_ref, ikg_ref, widx_ref = extra
    main_ref, iw_ref = out_refs
    q_tiles = A_Q_COLS // NMM_TN
    k_tile = q_tiles
    v_tile = k_tile + 1
    iq_tile0 = v_tile + 1
    idx_tile = A_MAIN_COLS // NMM_TN

    def store_main(fn):
        def emit(rows, y):
            main_ref[rows, :] = fn(y).astype(BF16)
        _row_chunks(h_ref, w_ref, emit)

    @pl.when(n < q_tiles)
    def _():
        store_main(lambda y: _head_norm(y, qg_ref[...], A_HEAD_DIM ** -0.5 * LOG2_E))

    @pl.when(n == k_tile)
    def _():
        store_main(lambda y: _head_norm(y, kg_ref[...], 1.0))

    @pl.when(n == v_tile)
    def _():
        store_main(lambda y: y)

    @pl.when(jnp.logical_and(n >= iq_tile0, n < idx_tile))
    def _():
        store_main(lambda y: y * IDX_DIM ** -0.5)

    @pl.when(n == idx_tile)
    def _():
        def emit(rows, z):
            lane = lax.broadcasted_iota(I32, z.shape, 1)
            ms = jnp.sum(jnp.where(lane < IDX_DIM, z * z, 0.0), axis=-1, keepdims=True) * (1.0 / IDX_DIM)
            ikn = z * lax.rsqrt(ms + EPS) * ikg_ref[...]
            main_ref[rows, :] = jnp.concatenate(
                [ikn, jnp.zeros((z.shape[0], NMM_TN - LANES), F32)], axis=1).astype(BF16)
            iw_ref[rows, :] = z * IDX_HEADS ** -0.5
        _row_chunks(h_ref, widx_ref, emit)


def _dsa_in_proj(x, gain, w_in, w_idx, layer, q_gain, k_gain, ik_gain_padded):
    s, d = x.shape
    tm = min(NMM_TM, s)
    n_tiles = A_MAIN_COLS // NMM_TN + 1
    ncols = n_tiles * NMM_TN
    vec = lambda i, n: (0, 0)
    return _norm_matmul(
        x, gain, w_in, layer, n_tiles,
        extra=(q_gain.reshape(1, A_HEAD_DIM), k_gain.reshape(1, A_HEAD_DIM), ik_gain_padded, w_idx),
        extra_specs=[pl.BlockSpec((1, A_HEAD_DIM), vec), pl.BlockSpec((1, A_HEAD_DIM), vec),
                     pl.BlockSpec((1, LANES), vec), pl.BlockSpec((None, d, LANES), lambda i, n: (layer, 0, 0))],
        out_shapes=(jax.ShapeDtypeStruct((s, ncols), BF16), jax.ShapeDtypeStruct((s, LANES), F32)),
        out_specs=(pl.BlockSpec((tm, NMM_TN), lambda i, n: (i, n)),
                   pl.BlockSpec((tm, LANES), lambda i, n: (i, 0))),
        epilogue=_dsa_in_epilogue,
    )


def _identity_epilogue(n, h_ref, w_ref, extra, out_refs):
    def emit(rows, y):
        out_refs[0][rows, :] = y
    _row_chunks(h_ref, w_ref, emit)


def _hgrn_in_proj(x, gain, w, layer):
    s = x.shape[0]
    tm = min(NMM_TM, s)
    return _norm_matmul(
        x, gain, w, layer, w.shape[-1] // HG_IN_TN, extra=(), extra_specs=[],
        out_shapes=jax.ShapeDtypeStruct((s, w.shape[-1]), F32),
        out_specs=pl.BlockSpec((tm, HG_IN_TN), lambda i, n: (i, n)),
        epilogue=_identity_epilogue, tn=HG_IN_TN,
    )


def _out_proj_body(a_ref, w_ref, r_ref, o_ref):
    o_ref[...] = r_ref[...] + jnp.dot(a_ref[...], w_ref[...], preferred_element_type=F32)


def _out_proj(a, w, layer, res):
    s, k = a.shape
    d = w.shape[-1]
    tm = min(OUT_TM, s)
    return pl.pallas_call(
        _out_proj_body,
        grid=(s // tm,),
        in_specs=[
            pl.BlockSpec((tm, k), lambda i: (i, 0)),
            _resident((None, k, d), lambda i: (layer, 0, 0)),
            pl.BlockSpec((tm, d), lambda i: (i, 0)),
        ],
        out_specs=pl.BlockSpec((tm, d), lambda i: (i, 0)),
        out_shape=jax.ShapeDtypeStruct((s, d), F32),
        compiler_params=_params("parallel"),
        name="out_proj",
    )(a, w, res)


def _t5_bucket_table():
    exact = REL_BUCKETS // 2
    d = np.arange(2 * QB)
    df = np.maximum(d, 1).astype(np.float64)
    val = np.log(df / exact) / math.log(REL_MAX_DIST / exact) * (REL_BUCKETS - exact)
    big = d >= exact
    frac = np.abs(val[big] - np.round(val[big]))
    assert np.all((frac > 1e-4) | (frac == 0.0))
    large = np.minimum(exact + np.floor(val + 1e-9).astype(np.int64), REL_BUCKETS - 1)
    return np.where(d < exact, d, large).astype(np.int32)


def _near_bucket_tiles():
    table = _t5_bucket_table()
    r = np.arange(QB)[:, None]
    t = np.arange(QB)[None, :]
    tiles = [table[np.maximum(n * QB + t - r, 0)] for n in range(2)]
    return np.stack(tiles).astype(np.int32)


def _bias_body(rel_ref, bkt_ref, o_ref):
    j = pl.program_id(1)
    bkt = bkt_ref[0]
    for g in range(A_GROUP):
        h = j * A_GROUP + g
        far = rel_ref[REL_BUCKETS - 1, h]
        acc = jnp.zeros(bkt.shape, F32)
        for b in range(REL_BUCKETS):
            acc = jnp.where(bkt == b, (rel_ref[b, h] - far) * LOG2_E, acc)
        o_ref[0, 0, :, g * QB:(g + 1) * QB] = acc


def _near_bias(rel_bias):
    bkt = jnp.asarray(_near_bucket_tiles())
    return pl.pallas_call(
        _bias_body,
        grid=(2, A_KV_HEADS),
        in_specs=[
            pl.BlockSpec(memory_space=pltpu.SMEM),
            pl.BlockSpec((1, QB, QB), lambda n, j: (n, 0, 0)),
        ],
        out_specs=pl.BlockSpec((1, 1, QB, A_GROUP * QB), lambda n, j: (n, j, 0, 0)),
        out_shape=jax.ShapeDtypeStruct((2, A_KV_HEADS, QB, A_GROUP * QB), F32),
        compiler_params=_params("arbitrary", "arbitrary"),
        name="near_bias",
    )(rel_bias, bkt)


def _bit_transpose32(words):
    a = list(words)
    for j, mask in ((16, 0x0000FFFF), (8, 0x00FF00FF), (4, 0x0F0F0F0F), (2, 0x33333333), (1, 0x55555555)):
        shift = jnp.full(a[0].shape, j, I32)
        for k in range(32):
            if k & j == 0:
                t = (lax.shift_right_logical(a[k], shift) ^ a[k + j]) & mask
                a[k + j] = a[k + j] ^ t
                a[k] = a[k] ^ lax.shift_left(t, shift)
    return a


def _attn_body(q_ref, iq_ref, iw_ref, k_ref, ik_ref, vt_ref, bias_ref, o_ref,
               iqt_ref, qt_ref, sc_ref, plane_ref, eq_ref, hit_ref, st_ref, acc_ref, m_ref, *, topk):
    i = pl.program_id(0)
    gq = A_GROUP * QB
    POS_BITS = (sc_ref.shape[0] - 1).bit_length()

    iqt = iq_ref[...].astype(F32).T
    iqt_ref[...] = jnp.zeros_like(iqt_ref)
    for h in range(IDX_HEADS):
        iqt_ref[0:IDX_DIM, h * QB:(h + 1) * QB] = iqt[h * IDX_DIM:(h + 1) * IDX_DIM, :].astype(BF16)
    qt = q_ref[...].astype(F32).T
    eye = (lax.broadcasted_iota(I32, (QB, QB), 0) == lax.broadcasted_iota(I32, (QB, QB), 1)).astype(BF16)
    for h in range(A_HEADS):
        qt_ref[0:A_HEAD_DIM, h * QB:(h + 1) * QB] = qt[h * A_HEAD_DIM:(h + 1) * A_HEAD_DIM, :].astype(BF16)
        qt_ref[A_HEAD_DIM:, h * QB:(h + 1) * QB] = eye
    iwt = iw_ref[...].T

    n_chunks = (i * QB + QB + SCORE_CHUNK - 1) // SCORE_CHUNK

    def score_chunk(c, causal_mask):
        row0 = pl.multiple_of(c * SCORE_CHUNK, SCORE_CHUNK)
        ikc = ik_ref[pl.ds(row0, SCORE_CHUNK), :]
        sc = jnp.zeros((SCORE_CHUNK, QB), F32)
        for hp in range(IDX_HEADS // 2):
            raw = jnp.dot(ikc, iqt_ref[:, hp * 2 * QB:(hp + 1) * 2 * QB], preferred_element_type=F32)
            for u in range(2):
                h = 2 * hp + u
                sc = sc + jnp.maximum(raw[:, u * QB:(u + 1) * QB], 0.0) * iwt[IDX_DIM + h:IDX_DIM + h + 1, :]
        bits = pltpu.bitcast(sc, I32)
        skey = jnp.where(bits >= 0, bits, bits ^ 0x7FFFFFFF)
        if causal_mask:
            key_pos = row0 + lax.broadcasted_iota(I32, sc.shape, 0)
            q_pos = i * QB + lax.broadcasted_iota(I32, sc.shape, 1)
            skey = jnp.where(key_pos <= q_pos, skey, INT_MIN)
        sc_ref[pl.ds(row0, SCORE_CHUNK), :] = skey
        ukey = skey ^ INT_MIN
        for gi in range(SCORE_CHUNK // PLANE_KEYS):
            base = gi * PLANE_KEYS
            planes = _bit_transpose32([ukey[base + 8 * j:base + 8 * (j + 1), :] for j in range(32)])
            for b in range(32):
                plane_ref[c * (SCORE_CHUNK // PLANE_KEYS) + gi, b] = planes[b]

    def full_chunk(c, carry):
        score_chunk(c, False)
        return carry

    lax.fori_loop(0, n_chunks - 1, full_chunk, 0)
    score_chunk(n_chunks - 1, True)

    eq_ref[...] = jnp.full_like(eq_ref, -1)
    hit_ref[...] = jnp.full_like(hit_ref, -1)
    groups_per_iter = SCORE_CHUNK // PLANE_KEYS

    def settle(g, keep_prev):
        hit = hit_ref[g]
        return jnp.where(keep_prev != 0, hit, eq_ref[g] ^ hit)

    def bit_pass(it, carry):
        chosen, n_greater, keep_prev = carry
        bit = 31 - it

        def sweep(gg, acc):
            for u in range(groups_per_iter):
                g = gg * groups_per_iter + u
                eq = settle(g, keep_prev)
                hit = eq & plane_ref[g, bit]
                eq_ref[g] = eq
                hit_ref[g] = hit
                acc = acc + lax.population_count(hit)
            return acc

        acc = lax.fori_loop(0, n_chunks, sweep, jnp.zeros((8, QB), I32))
        n_ge = n_greater + jnp.sum(acc, axis=0, keepdims=True)
        keep = n_ge >= topk
        chosen = jnp.where(keep, chosen | (jnp.int32(1) << bit), chosen)
        n_greater = jnp.where(keep, n_greater, n_ge)
        return chosen, n_greater, keep.astype(I32)

    zeros_row = jnp.zeros((1, QB), I32)
    chosen, n_greater, keep_last = lax.fori_loop(0, 32, bit_pass, (zeros_row, zeros_row, zeros_row + 1))
    thr = jnp.maximum(chosen ^ INT_MIN, INT_MIN + 1)

    def count_equal(gg, acc):
        for u in range(groups_per_iter):
            acc = acc + lax.population_count(settle(gg * groups_per_iter + u, keep_last))
        return acc

    n_equal = jnp.sum(lax.fori_loop(0, n_chunks, count_equal, jnp.zeros((8, QB), I32)), axis=0, keepdims=True)
    n_ge_thr = jnp.where(chosen == 0, 0, n_greater + n_equal)

    n_count = n_chunks * (SCORE_CHUNK // COUNT_ROWS)

    def count_keys(pred):
        def body(r, acc):
            row0 = pl.multiple_of(r * COUNT_ROWS, COUNT_ROWS)
            hit = jnp.where(pred(sc_ref[pl.ds(row0, COUNT_ROWS), :], row0), 1, 0).astype(I32)
            return acc + jnp.sum(hit.reshape(COUNT_ROWS // 8, 8, QB), axis=0)
        acc = lax.fori_loop(0, n_count, body, jnp.zeros((8, QB), I32))
        return jnp.sum(acc, axis=0, keepdims=True)

    @pl.when(jnp.max(n_ge_thr) > topk)
    def _():
        need = topk - n_greater

        def key_pos_of(blk, row0):
            return row0 + lax.broadcasted_iota(I32, blk.shape, 0)

        def pos_step(it, last):
            cand = last | (jnp.int32(1) << (POS_BITS - 1 - it))
            n_before = count_keys(lambda blk, row0: jnp.logical_and(blk == thr, key_pos_of(blk, row0) < cand))
            return jnp.where(n_before < need, cand, last)

        last = lax.fori_loop(0, POS_BITS, pos_step, jnp.zeros((1, QB), I32))

        def demote(r, carry):
            row0 = pl.multiple_of(r * COUNT_ROWS, COUNT_ROWS)
            blk = sc_ref[pl.ds(row0, COUNT_ROWS), :]
            late_tie = jnp.logical_and(blk == thr, key_pos_of(blk, row0) > last)
            sc_ref[pl.ds(row0, COUNT_ROWS), :] = jnp.where(late_tie, thr - 1, blk)
            return carry

        lax.fori_loop(0, n_count, demote, 0)

    m_ref[...] = jnp.full_like(m_ref, M_INIT)
    acc_ref[...] = jnp.zeros_like(acc_ref)
    heads = range(A_KV_HEADS)

    def logits(c, slot):
        row0 = pl.multiple_of(c * KB, KB)
        sel = sc_ref[pl.ds(row0, KB), :] >= thr
        mask_cols = jnp.where(sel, 0.0, MASKED_LOGIT).astype(BF16)
        kblk = k_ref[pl.ds(row0, KB), :]
        for j in heads:
            kaug = jnp.concatenate([kblk[:, j * A_HEAD_DIM:(j + 1) * A_HEAD_DIM], mask_cols], axis=1)
            st_ref[slot, j] = jnp.dot(kaug, qt_ref[:, j * gq:(j + 1) * gq], preferred_element_type=F32)

    def softmax_pv(c, slot, near):
        ps, alphas = [], []
        for j in heads:
            st = st_ref[slot, j]
            if near != (None, None):
                halves = [st[0:QB], st[QB:KB]]
                halves = [hv if n is None else hv + bias_ref[n, j] for hv, n in zip(halves, near)]
                st = jnp.concatenate(halves, axis=0)
            m_old = m_ref[j]
            m_new = jnp.maximum(m_old, jnp.max(st, axis=0, keepdims=True))
            m_ref[j] = m_new
            ps.append(jnp.exp2(st - m_new).astype(BF16))
            alphas.append(jnp.exp2(m_old - m_new))
        vblk = vt_ref[c]
        pvs = [jnp.dot(vblk[j * VT_ROWS:(j + 1) * VT_ROWS, :], ps[j], preferred_element_type=F32)
               for j in heads]
        for j in heads:
            acc_ref[j] = acc_ref[j] * alphas[j] + pvs[j]

    far = (None, None)
    n_far = jnp.maximum(lax.shift_right_arithmetic(i - 1, 1), 0)
    i_odd = (i & 1) == 1

    @pl.when(i >= 1)
    def _():
        logits(0, 0)

    @pl.when(i == 0)
    def _():
        logits(0, 1)

    def far_pair(pp, carry):
        c0 = 2 * pp
        logits(c0 + 1, 1)
        softmax_pv(c0, 0, far)
        logits(c0 + 2, 0)
        softmax_pv(c0 + 1, 1, far)
        return carry

    lax.fori_loop(0, lax.shift_right_arithmetic(n_far, 1), far_pair, 0)

    @pl.when((n_far & 1) == 1)
    def _():
        softmax_pv(n_far - 1, 0, far)
        logits(n_far, 0)

    @pl.when(i_odd)
    def _():
        softmax_pv(n_far, 0, (1, 0))

    @pl.when(jnp.logical_and(jnp.logical_not(i_odd), i >= 2))
    def _():
        logits(n_far + 1, 1)
        softmax_pv(n_far, 0, (None, 1))

    @pl.when(jnp.logical_not(i_odd))
    def _():
        softmax_pv(lax.shift_right_arithmetic(i, 1), 1, (0, None))

    for j in heads:
        acc = acc_ref[j]
        out_t = acc[0:A_HEAD_DIM, :] * (1.0 / acc[A_HEAD_DIM:A_HEAD_DIM + 1, :])
        for g in range(A_GROUP):
            h = j * A_GROUP + g
            o_ref[:, h * A_HEAD_DIM:(h + 1) * A_HEAD_DIM] = out_t[:, g * QB:(g + 1) * QB].T.astype(o_ref.dtype)


def _dsa_attention(main, iw, vt3, bias, topk):
    s = main.shape[0]
    nb = s // QB
    q_blk = A_Q_COLS // A_Q_COLS
    del q_blk
    return pl.pallas_call(
        functools.partial(_attn_body, topk=topk),
        grid=(nb,),
        in_specs=[
            pl.BlockSpec((QB, A_Q_COLS), lambda i: (i, 0)),
            pl.BlockSpec((QB, A_IQ_COLS), lambda i: (i, (A_Q_COLS + 2 * A_KV_COLS) // A_IQ_COLS)),
            pl.BlockSpec((QB, LANES), lambda i: (i, 0)),
            _resident((s, A_KV_COLS), lambda i: (0, A_Q_COLS // A_KV_COLS)),
            _resident((s, LANES), lambda i: (0, A_MAIN_COLS // LANES)),
            _resident((s // KB, A_KV_HEADS * VT_ROWS, KB), lambda i: (0, 0, 0)),
            _resident((2, A_KV_HEADS, QB, A_GROUP * QB), lambda i: (0, 0, 0, 0)),
        ],
        out_specs=pl.BlockSpec((QB, A_Q_COLS), lambda i: (i, 0)),
        out_shape=jax.ShapeDtypeStruct((s, A_Q_COLS), BF16),
        scratch_shapes=[
            pltpu.VMEM((LANES, IDX_HEADS * QB), BF16),
            pltpu.VMEM((A_HEAD_DIM + QB, A_HEADS * QB), BF16),
            pltpu.VMEM((s, QB), I32),
            pltpu.VMEM((s // PLANE_KEYS, 32, 8, QB), I32),
            pltpu.VMEM((s // PLANE_KEYS, 8, QB), I32),
            pltpu.VMEM((s // PLANE_KEYS, 8, QB), I32),
            pltpu.VMEM((2, A_KV_HEADS, KB, A_GROUP * QB), F32),
            pltpu.VMEM((A_KV_HEADS, VT_ROWS, A_GROUP * QB), F32),
            pltpu.VMEM((A_KV_HEADS, 1, A_GROUP * QB), F32),
        ],
        compiler_params=_params("arbitrary"),
        name="dsa_attention",
    )(main, main, iw, main, main, vt3, bias)


def _dsa_mixer(x, gain, w_in, w_idx, w_out, layer, q_gain, k_gain, ik_gain_padded, bias):
    s = x.shape[0]
    topk = min(TOPK_MAX, s // 4)
    main, iw = _dsa_in_proj(x, gain, w_in, w_idx, layer, q_gain, k_gain, ik_gain_padded)
    v = main[:, A_Q_COLS + A_KV_COLS:A_Q_COLS + 2 * A_KV_COLS]
    vt4 = v.reshape(s // KB, KB, A_KV_HEADS, A_HEAD_DIM).transpose(0, 2, 3, 1)
    tail = jnp.zeros((s // KB, A_KV_HEADS, VT_ROWS - A_HEAD_DIM, KB), BF16).at[:, :, 0, :].set(1.0)
    vt3 = jnp.concatenate([vt4, tail], axis=2).reshape(s // KB, A_KV_HEADS * VT_ROWS, KB)
    o = _dsa_attention(main, iw, vt3, bias, topk)
    return _out_proj(o, w_out, layer, x)


def _hgrn_sum_matrices():
    c = HG_CHUNK
    t = np.arange(c)[:, None]
    s = np.arange(c)[None, :]
    mats = []
    for l in range(1, HG_LEVELS + 1):
        blk = 1 << l
        m = (t // blk) * blk + blk // 2
        upper = t >= m
        mats.append(np.where(upper, (s >= m) & (s <= t), (s > t) & (s <= m - 1)))
    mats.append(s <= t)
    mats.append(s > t)
    return np.concatenate(mats, axis=0).astype(np.float32)


def _hgrn_split_levels():
    t = np.arange(HG_CHUNK)[:, None]
    s = np.arange(HG_CHUNK)[None, :]
    lvl = np.zeros((HG_CHUNK, HG_CHUNK), np.int32)
    for l in range(1, HG_LEVELS + 1):
        lvl[(s < t) & ((t >> l) == (s >> l)) & (((t >> (l - 1)) & 1) == 1) & (((s >> (l - 1)) & 1) == 0)] = l
    assert np.all((lvl > 0) == (s < t))
    return lvl


def _hgrn_body(q_ref, f_ref, i_ref, g_ref, lbl_ref, og_ref, gm_ref, split_ref, o_ref, st_ref, *, layer):
    c = pl.program_id(1)
    ch = HG_CHUNK

    @pl.when(c == 0)
    def _():
        st_ref[...] = jnp.zeros_like(st_ref)

    logits = lbl_ref[...]
    e = jnp.exp(logits - jnp.max(logits, axis=0, keepdims=True))
    p = e / jnp.sum(e, axis=0, keepdims=True)
    csum = p[0:1, :]
    for r in range(1, layer + 1):
        csum = csum + p[r:r + 1, :]
    lb = csum - p[0:1, :]

    def chunk(rows):
        fgate = lb + (1.0 - lb) * jax.nn.sigmoid(f_ref[rows, :])
        kin = 1.0 - fgate
        logf = jnp.log2(fgate)
        logf_hi = logf.astype(BF16)
        logf_lo = (logf - logf_hi.astype(F32)).astype(BF16)
        logf_split = jnp.concatenate([logf_hi, logf_lo], axis=0)
        decs = [jnp.dot(gm_ref[r * ch:(r + 1) * ch, :], logf_split, preferred_element_type=F32)
                for r in range(HG_LEVELS + 2)]

        row = lax.broadcasted_iota(I32, (ch, B_KEY_DIM), 0)
        split = split_ref[...]
        nt = (((1,), (1,)), ((), ()))
        heads = range(HG_HEADS_PER_STEP)
        levels = range(1, HG_LEVELS + 1)
        sls = [slice(hh * B_KEY_DIM, (hh + 1) * B_KEY_DIM) for hh in heads]

        qs = [q_ref[rows, sl] * (B_KEY_DIM ** -0.5) for sl in sls]
        ks = [kin[:, sl] for sl in sls]
        vs = [i_ref[rows, sl] for sl in sls]
        b_incl = [decs[HG_LEVELS][:, sl] for sl in sls]
        b_rest = [decs[HG_LEVELS + 1][:, sl] for sl in sls]
        states = [st_ref[hh] for hh in heads]

        zs = []
        for hh in heads:
            for l in levels:
                upper = (row & (1 << (l - 1))) != 0
                ex = jnp.exp2(decs[l - 1][:, sls[hh]])
                zs.append((jnp.where(upper, qs[hh], ks[hh]) * ex).astype(BF16))
        prods = [lax.dot_general(z, z, nt, preferred_element_type=F32) for z in zs]
        o_inter = [lax.dot_general((qs[hh] * jnp.exp2(b_incl[hh])).astype(BF16), states[hh].astype(BF16), nt,
                                   preferred_element_type=F32) for hh in heads]
        new_kv = [jnp.dot(vs[hh].T.astype(BF16), (ks[hh] * jnp.exp2(b_rest[hh])).astype(BF16),
                          preferred_element_type=F32) for hh in heads]
        a_mats = []
        for hh in heads:
            a = jnp.zeros((ch, ch), F32)
            for l in levels:
                a = jnp.where(split == l, prods[hh * HG_LEVELS + l - 1], a)
            a_mats.append(a.astype(BF16))
        o_intra = [jnp.dot(a_mats[hh], vs[hh].astype(BF16), preferred_element_type=F32) for hh in heads]

        for hh in heads:
            sl = sls[hh]
            st_ref[hh] = states[hh] * jnp.exp2(b_incl[hh][ch - 1:ch, :]) + new_kv[hh]
            o = o_inter[hh] + o_intra[hh] + jnp.sum(qs[hh] * ks[hh], axis=-1, keepdims=True) * vs[hh]
            on = o * _rms_scale(o) * og_ref[:, sl]
            gate = g_ref[rows, sl]
            o_ref[rows, sl] = (on * (gate * jax.nn.sigmoid(gate))).astype(o_ref.dtype)

    for cc in range(HG_CHUNKS_PER_STEP):
        chunk(slice(cc * ch, (cc + 1) * ch))


def _hgrn_core(proj, lb_logits, o_gain, layer):
    s = proj.shape[0]
    w = HG_HEADS_PER_STEP * B_KEY_DIM
    nhb = D_MODEL // w
    rows_per_step = HG_CHUNKS_PER_STEP * HG_CHUNK
    gm = jnp.asarray(np.tile(_hgrn_sum_matrices(), (1, 2)), dtype=BF16)
    split = jnp.asarray(_hgrn_split_levels())
    col = lambda off: (lambda hb, c: (c, off * nhb + hb))
    return pl.pallas_call(
        functools.partial(_hgrn_body, layer=layer),
        grid=(nhb, s // rows_per_step),
        in_specs=[
            pl.BlockSpec((rows_per_step, w), col(0)),
            pl.BlockSpec((rows_per_step, w), col(1)),
            pl.BlockSpec((rows_per_step, w), col(2)),
            pl.BlockSpec((rows_per_step, w), col(3)),
            pl.BlockSpec((DEPTH, w), lambda hb, c: (0, hb)),
            pl.BlockSpec((1, w), lambda hb, c: (0, hb)),
            pl.BlockSpec(gm.shape, lambda hb, c: (0, 0)),
            pl.BlockSpec(split.shape, lambda hb, c: (0, 0)),
        ],
        out_specs=pl.BlockSpec((rows_per_step, w), lambda hb, c: (c, hb)),
        out_shape=jax.ShapeDtypeStruct((s, D_MODEL), BF16),
        scratch_shapes=[pltpu.VMEM((HG_HEADS_PER_STEP, B_KEY_DIM, B_KEY_DIM), F32)],
        compiler_params=_params("parallel", "arbitrary"),
        name="hgrn_core",
    )(proj, proj, proj, proj, lb_logits, o_gain.reshape(1, D_MODEL), gm, split)


def _hgrn_mixer(x, gain, w_in, w_out, mixer_idx, lb_logits, o_gain, layer):
    proj = _hgrn_in_proj(x, gain, w_in, mixer_idx)
    o = _hgrn_core(proj, lb_logits, o_gain, layer)
    return _out_proj(o, w_out, mixer_idx, x)


def kernel(x, norm_gains, ffn_w_gate, ffn_w_up, ffn_w_down, dsa_w_in, dsa_w_out, dsa_q_gain, dsa_k_gain,
           dsa_idx_k_gain, rel_bias, hgrn_w_in, hgrn_w_out, hgrn_lb_logits, hgrn_o_gain):
    assert x.shape[0] == 1 and x.shape[2] == D_MODEL and x.shape[1] % FFN_TM == 0
    h = x.reshape(x.shape[1], D_MODEL)
    dsa_w_idx = jnp.pad(dsa_w_in[:, :, A_MAIN_COLS:], ((0, 0), (0, 0), (0, LANES - (A_IN_COLS - A_MAIN_COLS))))
    dsa_wo = dsa_w_out.astype(BF16)
    ik_gain = jnp.pad(dsa_idx_k_gain, ((0, 0), (0, LANES - IDX_DIM)))
    hg_wo = hgrn_w_out.astype(BF16)
    bias = _near_bias(rel_bias)

    for layer in range(DEPTH):
        j = layer // N_MIXERS
        h = _ffn(h, norm_gains[layer, 0], ffn_w_gate, ffn_w_up, ffn_w_down, (layer, 0))
        if layer % N_MIXERS == 0:
            h = _dsa_mixer(h, norm_gains[layer, 1], dsa_w_in, dsa_w_idx, dsa_wo, j, dsa_q_gain[j], dsa_k_gain[j],
                           ik_gain[j:j + 1], bias)
        else:
            h = _hgrn_mixer(h, norm_gains[layer, 1], hgrn_w_in, hg_wo, j, hgrn_lb_logits, hgrn_o_gain[j], layer)
        h = _ffn(h, norm_gains[layer, 2], ffn_w_gate, ffn_w_up, ffn_w_down, (layer, 1))
    return h.reshape(x.shape)
```

```python
import functools
import math

import numpy as np
import jax
import jax.numpy as jnp
from jax import lax
from jax.experimental import pallas as pl
from jax.experimental.pallas import tpu as pltpu

F32 = jnp.float32
BF16 = jnp.bfloat16
I32 = jnp.int32

D_MODEL = 2048
DEPTH = 4
N_MIXERS = 2
A_HEADS = 16
A_HEAD_DIM = 128
A_KV_HEADS = 4
A_GROUP = A_HEADS // A_KV_HEADS
IDX_HEADS = 16
IDX_DIM = 64
TOPK_MAX = 256
REL_BUCKETS = 32
REL_MAX_DIST = 128
B_KEY_DIM = 128
B_HEADS = D_MODEL // B_KEY_DIM
D_FF = 5632
EPS = 1e-6

A_Q_COLS = A_HEADS * A_HEAD_DIM
A_KV_COLS = A_KV_HEADS * A_HEAD_DIM
A_IQ_COLS = IDX_HEADS * IDX_DIM
A_MAIN_COLS = A_Q_COLS + 2 * A_KV_COLS + A_IQ_COLS
A_IN_COLS = A_MAIN_COLS + IDX_DIM + IDX_HEADS

LANES = 128
V7X_VMEM_BYTES = 64 * 1024 * 1024
VMEM_LIMIT_BYTES = 56 * 1024 * 1024
FFN_REST_VMEM_LIMIT_BYTES = 58 * 1024 * 1024

INT_MIN = -(2 ** 31)
LOG2_E = math.log2(math.e)
MASKED_LOGIT = -1e30
M_INIT = -1e20
VT_ROWS = A_HEAD_DIM + 16

FFN_TM = 1024
FFN_TF = 512
FFN_TF_FIRST = 256
FFN_COPY_ROWS = 128
NMM_TM = 1024
NMM_TN = 512
HG_IN_TN = 1024
NMM_ROWS = 256
OUT_TM = 512
QB = 128
KB = 2 * QB
SCORE_CHUNK = 512
PLANE_KEYS = 256
COUNT_ROWS = 512
HG_CHUNK = 128
HG_HEADS_PER_STEP = 4
HG_CHUNKS_PER_STEP = 4
HG_LEVELS = 7


def _params(*semantics, vmem_limit_bytes=VMEM_LIMIT_BYTES):
    return pltpu.CompilerParams(dimension_semantics=semantics, vmem_limit_bytes=vmem_limit_bytes)


def _resident(shape, index_map):
    return pl.BlockSpec(shape, index_map, pipeline_mode=pl.Buffered(1))


def _rms_scale(x):
    return lax.rsqrt(jnp.mean(x * x, axis=-1, keepdims=True) + EPS)


def _ffn_step(j, n_steps, x_ref, g_ref, weights, o_ref, h_ref):
    @pl.when(j == 0)
    def _():
        x = x_ref[...]
        h_ref[...] = (x * _rms_scale(x) * g_ref[...]).astype(BF16)
        o_ref[...] = jnp.zeros_like(o_ref)

    h = h_ref[...]
    wg, wu, wd = weights()
    gate = jnp.dot(h, wg, preferred_element_type=F32)
    up = jnp.dot(h, wu, preferred_element_type=F32)
    act = (gate * jax.nn.sigmoid(gate) * up).astype(BF16)
    o_ref[...] += jnp.dot(act, wd, preferred_element_type=F32)

    @pl.when(j == n_steps - 1)
    def _():
        o_ref[...] = x_ref[...] + 0.5 * o_ref[...]


def _ffn_first_body(x_ref, g_ref, wg_ref, wu_ref, wd_ref, y_ref, wgb_ref, wub_ref, wdb_ref, h_ref):
    def weights():
        tiles = []
        for src, dst in ((wg_ref, wgb_ref), (wu_ref, wub_ref), (wd_ref, wdb_ref)):
            tiles.append(src[...].astype(BF16))
            dst[...] = tiles[-1]
        return tiles

    _ffn_step(pl.program_id(0), pl.num_programs(0), x_ref, g_ref, weights, y_ref, h_ref)


def _ffn_rest_body(x_ref, y0_ref, g_ref, wg_ref, wu_ref, wd_ref, o_ref, h_ref):
    i = pl.program_id(0)
    j = pl.program_id(1)

    @pl.when(jnp.logical_and(i == 0, j < o_ref.shape[0] // FFN_COPY_ROWS))
    def _():
        o_ref[pl.ds(pl.multiple_of(j * FFN_COPY_ROWS, FFN_COPY_ROWS), FFN_COPY_ROWS), :] = y0_ref[...]

    @pl.when(i > 0)
    def _():
        _ffn_step(j, pl.num_programs(1), x_ref, g_ref, lambda: (wg_ref[...], wu_ref[...], wd_ref[...]), o_ref, h_ref)


def _ffn(x, gain, wg32, wu32, wd32, lead):
    s, d = x.shape
    f = wg32.shape[-1]
    tm = min(FFN_TM, s)
    squeeze = (None,) * len(lead)
    n_copy = tm // FFN_COPY_ROWS
    assert f // FFN_TF >= n_copy
    y0, wg, wu, wd = pl.pallas_call(
        _ffn_first_body,
        grid=(f // FFN_TF_FIRST,),
        in_specs=[
            _resident((tm, d), lambda j: (0, 0)),
            pl.BlockSpec((1, d), lambda j: (0, 0)),
            pl.BlockSpec(squeeze + (d, FFN_TF_FIRST), lambda j: lead + (0, j)),
            pl.BlockSpec(squeeze + (d, FFN_TF_FIRST), lambda j: lead + (0, j)),
            pl.BlockSpec(squeeze + (FFN_TF_FIRST, d), lambda j: lead + (j, 0)),
        ],
        out_specs=(
            pl.BlockSpec((tm, d), lambda j: (0, 0)),
            pl.BlockSpec((d, FFN_TF_FIRST), lambda j: (0, j)),
            pl.BlockSpec((d, FFN_TF_FIRST), lambda j: (0, j)),
            pl.BlockSpec((FFN_TF_FIRST, d), lambda j: (j, 0)),
        ),
        out_shape=(jax.ShapeDtypeStruct((tm, d), F32), jax.ShapeDtypeStruct((d, f), BF16),
                   jax.ShapeDtypeStruct((d, f), BF16), jax.ShapeDtypeStruct((f, d), BF16)),
        scratch_shapes=[pltpu.VMEM((tm, d), BF16)],
        compiler_params=_params("arbitrary"),
        name="ffn_first",
    )(x, gain.reshape(1, d), wg32, wu32, wd32)
    wcol = lambda i, j: (0, jnp.where(i > 0, j, 0))
    wrow = lambda i, j: (jnp.where(i > 0, j, 0), 0)
    return pl.pallas_call(
        _ffn_rest_body,
        grid=(s // tm, f // FFN_TF),
        in_specs=[
            pl.BlockSpec((tm, d), lambda i, j: (jnp.minimum(jnp.maximum(i, 1), s // tm - 1), 0)),
            pl.BlockSpec((FFN_COPY_ROWS, d), lambda i, j: (jnp.where(i == 0, jnp.minimum(j, n_copy - 1), n_copy - 1), 0)),
            pl.BlockSpec((1, d), lambda i, j: (0, 0)),
            pl.BlockSpec((d, FFN_TF), wcol),
            pl.BlockSpec((d, FFN_TF), wcol),
            pl.BlockSpec((FFN_TF, d), wrow),
        ],
        out_specs=pl.BlockSpec((tm, d), lambda i, j: (i, 0)),
        out_shape=jax.ShapeDtypeStruct((s, d), F32),
        scratch_shapes=[pltpu.VMEM((tm, d), BF16)],
        compiler_params=_params("arbitrary", "arbitrary", vmem_limit_bytes=FFN_REST_VMEM_LIMIT_BYTES),
        name="ffn_rest",
    )(x, y0, gain.reshape(1, d), wg, wu, wd)


def _nmm_body(*refs, n_extra, epilogue):
    x_ref, g_ref, w_ref = refs[:3]
    extra = refs[3:3 + n_extra]
    out_refs = refs[3 + n_extra:-1]
    h_ref = refs[-1]
    n = pl.program_id(1)

    @pl.when(n == 0)
    def _():
        x = x_ref[...]
        h_ref[...] = (x * _rms_scale(x) * g_ref[...]).astype(BF16)

    epilogue(n, h_ref, w_ref, extra, out_refs)


assert NMM_ROWS == KB


def _row_chunks(h_ref, w_ref, emit):
    w = w_ref[...].astype(BF16)
    for r in range(h_ref.shape[0] // NMM_ROWS):
        rows = slice(r * NMM_ROWS, (r + 1) * NMM_ROWS)
        emit(rows, jnp.dot(h_ref[rows, :], w, preferred_element_type=F32))


def _norm_matmul(x, gain, w, layer, n_tiles, extra, extra_specs, out_shapes, out_specs, epilogue, tn=NMM_TN):
    s, d = x.shape
    tm = min(NMM_TM, s)
    last_whole = w.shape[-1] // tn - 1
    return pl.pallas_call(
        functools.partial(_nmm_body, n_extra=len(extra), epilogue=epilogue),
        grid=(s // tm, n_tiles),
        in_specs=[
            pl.BlockSpec((tm, d), lambda i, n: (i, 0)),
            pl.BlockSpec((1, d), lambda i, n: (0, 0)),
            pl.BlockSpec((None, d, tn), lambda i, n: (layer, 0, jnp.minimum(n, last_whole))),
        ] + list(extra_specs),
        out_specs=out_specs,
        out_shape=out_shapes,
        scratch_shapes=[pltpu.VMEM((tm, d), BF16)],
        compiler_params=_params("parallel", "arbitrary"),
        name="norm_matmul",
    )(x, gain.reshape(1, d), w, *extra)


def _head_norm(y, gain, scale):
    parts = []
    for c in range(NMM_TN // A_HEAD_DIM):
        z = y[:, c * A_HEAD_DIM:(c + 1) * A_HEAD_DIM]
        parts.append(z * _rms_scale(z) * (gain * scale))
    return jnp.concatenate(parts, axis=1)


def _dsa_in_epilogue(n, h_ref, w_ref, extra, out_refs):
    qg_ref, kg_ref, ikg_ref, widx_ref = extra
    main_ref, iw_ref, vt_ref = out_refs
    q_tiles = A_Q_COLS // NMM_TN
    k_tile = q_tiles
    v_tile = k_tile + 1
    iq_tile0 = v_tile + 1
    idx_tile = A_MAIN_COLS // NMM_TN

    def store_main(fn):
        def emit(rows, y):
            main_ref[rows, :] = fn(y).astype(BF16)
        _row_chunks(h_ref, w_ref, emit)

    @pl.when(n < q_tiles)
    def _():
        store_main(lambda y: _head_norm(y, qg_ref[...], A_HEAD_DIM ** -0.5 * LOG2_E))

    @pl.when(n == k_tile)
    def _():
        store_main(lambda y: _head_norm(y, kg_ref[...], 1.0))

    @pl.when(n == v_tile)
    def _():
        tail = (lax.broadcasted_iota(I32, (VT_ROWS - A_HEAD_DIM, KB), 0) == 0).astype(BF16)

        def emit(rows, y):
            main_ref[rows, :] = y.astype(BF16)
            blk = rows.start // KB
            for j in range(A_KV_HEADS):
                vt_ref[blk, j * VT_ROWS:j * VT_ROWS + A_HEAD_DIM, :] = (
                    y[:, j * A_HEAD_DIM:(j + 1) * A_HEAD_DIM].T.astype(BF16))
                vt_ref[blk, j * VT_ROWS + A_HEAD_DIM:(j + 1) * VT_ROWS, :] = tail
        _row_chunks(h_ref, w_ref, emit)

    @pl.when(jnp.logical_and(n >= iq_tile0, n < idx_tile))
    def _():
        store_main(lambda y: y * IDX_DIM ** -0.5)

    @pl.when(n == idx_tile)
    def _():
        def emit(rows, z):
            lane = lax.broadcasted_iota(I32, z.shape, 1)
            ms = jnp.sum(jnp.where(lane < IDX_DIM, z * z, 0.0), axis=-1, keepdims=True) * (1.0 / IDX_DIM)
            ikn = z * lax.rsqrt(ms + EPS) * ikg_ref[...]
            main_ref[rows, :] = jnp.concatenate(
                [ikn, jnp.zeros((z.shape[0], NMM_TN - LANES), F32)], axis=1).astype(BF16)
            iw_ref[rows, :] = z * IDX_HEADS ** -0.5
        _row_chunks(h_ref, widx_ref, emit)


def _dsa_in_proj(x, gain, w_in, w_idx, layer, q_gain, k_gain, ik_gain_padded):
    s, d = x.shape
    tm = min(NMM_TM, s)
    n_tiles = A_MAIN_COLS // NMM_TN + 1
    ncols = n_tiles * NMM_TN
    vec = lambda i, n: (0, 0)
    return _norm_matmul(
        x, gain, w_in, layer, n_tiles,
        extra=(q_gain.reshape(1, A_HEAD_DIM), k_gain.reshape(1, A_HEAD_DIM), ik_gain_padded, w_idx),
        extra_specs=[pl.BlockSpec((1, A_HEAD_DIM), vec), pl.BlockSpec((1, A_HEAD_DIM), vec),
                     pl.BlockSpec((1, LANES), vec), pl.BlockSpec((None, d, LANES), lambda i, n: (layer, 0, 0))],
        out_shapes=(jax.ShapeDtypeStruct((s, ncols), BF16), jax.ShapeDtypeStruct((s, LANES), F32),
                    jax.ShapeDtypeStruct((s // KB, A_KV_HEADS * VT_ROWS, KB), BF16)),
        out_specs=(pl.BlockSpec((tm, NMM_TN), lambda i, n: (i, n)),
                   pl.BlockSpec((tm, LANES), lambda i, n: (i, 0)),
                   pl.BlockSpec((tm // KB, A_KV_HEADS * VT_ROWS, KB), lambda i, n: (i, 0, 0))),
        epilogue=_dsa_in_epilogue,
    )


def _identity_epilogue(n, h_ref, w_ref, extra, out_refs):
    def emit(rows, y):
        out_refs[0][rows, :] = y
    _row_chunks(h_ref, w_ref, emit)


def _hgrn_in_proj(x, gain, w, layer):
    s = x.shape[0]
    tm = min(NMM_TM, s)
    return _norm_matmul(
        x, gain, w, layer, w.shape[-1] // HG_IN_TN, extra=(), extra_specs=[],
        out_shapes=jax.ShapeDtypeStruct((s, w.shape[-1]), F32),
        out_specs=pl.BlockSpec((tm, HG_IN_TN), lambda i, n: (i, n)),
        epilogue=_identity_epilogue, tn=HG_IN_TN,
    )


def _out_proj_body(a_ref, w_ref, r_ref, o_ref, wb_ref):
    @pl.when(pl.program_id(0) == 0)
    def _():
        wb_ref[...] = w_ref[...].astype(BF16)

    o_ref[...] = r_ref[...] + jnp.dot(a_ref[...], wb_ref[...], preferred_element_type=F32)


def _out_proj(a, w, layer, res):
    s, k = a.shape
    d = w.shape[-1]
    tm = min(OUT_TM, s)
    return pl.pallas_call(
        _out_proj_body,
        grid=(s // tm,),
        in_specs=[
            pl.BlockSpec((tm, k), lambda i: (i, 0)),
            _resident((None, k, d), lambda i: (layer, 0, 0)),
            pl.BlockSpec((tm, d), lambda i: (i, 0)),
        ],
        out_specs=pl.BlockSpec((tm, d), lambda i: (i, 0)),
        out_shape=jax.ShapeDtypeStruct((s, d), F32),
        scratch_shapes=[pltpu.VMEM((k, d), BF16)],
        compiler_params=_params("arbitrary"),
        name="out_proj",
    )(a, w, res)


def _t5_bucket_table():
    exact = REL_BUCKETS // 2
    d = np.arange(2 * QB)
    df = np.maximum(d, 1).astype(np.float64)
    val = np.log(df / exact) / math.log(REL_MAX_DIST / exact) * (REL_BUCKETS - exact)
    big = d >= exact
    frac = np.abs(val[big] - np.round(val[big]))
    assert np.all((frac > 1e-4) | (frac == 0.0))
    large = np.minimum(exact + np.floor(val + 1e-9).astype(np.int64), REL_BUCKETS - 1)
    return np.where(d < exact, d, large).astype(np.int32)


def _near_bucket_tiles():
    table = _t5_bucket_table()
    r = np.arange(QB)[:, None]
    t = np.arange(QB)[None, :]
    tiles = [table[np.maximum(n * QB + t - r, 0)] for n in range(2)]
    return np.stack(tiles).astype(np.int32)


def _bias_body(rel_ref, bkt_ref, o_ref):
    j = pl.program_id(1)
    bkt = bkt_ref[0]
    for g in range(A_GROUP):
        h = j * A_GROUP + g
        far = rel_ref[REL_BUCKETS - 1, h]
        acc = jnp.zeros(bkt.shape, F32)
        for b in range(REL_BUCKETS):
            acc = jnp.where(bkt == b, (rel_ref[b, h] - far) * LOG2_E, acc)
        o_ref[0, 0, :, g * QB:(g + 1) * QB] = acc


def _near_bias(rel_bias):
    bkt = jnp.asarray(_near_bucket_tiles())
    return pl.pallas_call(
        _bias_body,
        grid=(2, A_KV_HEADS),
        in_specs=[
            pl.BlockSpec(memory_space=pltpu.SMEM),
            pl.BlockSpec((1, QB, QB), lambda n, j: (n, 0, 0)),
        ],
        out_specs=pl.BlockSpec((1, 1, QB, A_GROUP * QB), lambda n, j: (n, j, 0, 0)),
        out_shape=jax.ShapeDtypeStruct((2, A_KV_HEADS, QB, A_GROUP * QB), F32),
        compiler_params=_params("arbitrary", "arbitrary"),
        name="near_bias",
    )(rel_bias, bkt)


def _bit_transpose32(words):
    a = list(words)
    for j, mask in ((16, 0x0000FFFF), (8, 0x00FF00FF), (4, 0x0F0F0F0F), (2, 0x33333333), (1, 0x55555555)):
        shift = jnp.full(a[0].shape, j, I32)
        for k in range(32):
            if k & j == 0:
                t = (lax.shift_right_logical(a[k], shift) ^ a[k + j]) & mask
                a[k + j] = a[k + j] ^ t
                a[k] = a[k] ^ lax.shift_left(t, shift)
    return a


def _attn_body(q_ref, iq_ref, iw_ref, k_ref, ik_ref, vt_ref, bias_ref, o_ref,
               iqt_ref, qt_ref, sc_ref, plane_ref, eq_ref, hit_ref, st_ref, acc_ref, m_ref, *, topk):
    i = pl.program_id(0)
    gq = A_GROUP * QB
    POS_BITS = (sc_ref.shape[0] - 1).bit_length()

    iqt = iq_ref[...].astype(F32).T
    iqt_ref[...] = jnp.zeros_like(iqt_ref)
    for h in range(IDX_HEADS):
        iqt_ref[0:IDX_DIM, h * QB:(h + 1) * QB] = iqt[h * IDX_DIM:(h + 1) * IDX_DIM, :].astype(BF16)
    qt = q_ref[...].astype(F32).T
    eye = (lax.broadcasted_iota(I32, (QB, QB), 0) == lax.broadcasted_iota(I32, (QB, QB), 1)).astype(BF16)
    for h in range(A_HEADS):
        qt_ref[0:A_HEAD_DIM, h * QB:(h + 1) * QB] = qt[h * A_HEAD_DIM:(h + 1) * A_HEAD_DIM, :].astype(BF16)
        qt_ref[A_HEAD_DIM:, h * QB:(h + 1) * QB] = eye
    iwt = iw_ref[...].T

    n_chunks = (i * QB + QB + SCORE_CHUNK - 1) // SCORE_CHUNK

    def score_chunk(c, causal_mask):
        row0 = pl.multiple_of(c * SCORE_CHUNK, SCORE_CHUNK)
        ikc = ik_ref[pl.ds(row0, SCORE_CHUNK), :]
        sc = jnp.zeros((SCORE_CHUNK, QB), F32)
        for hp in range(IDX_HEADS // 2):
            raw = jnp.dot(ikc, iqt_ref[:, hp * 2 * QB:(hp + 1) * 2 * QB], preferred_element_type=F32)
            for u in range(2):
                h = 2 * hp + u
                sc = sc + jnp.maximum(raw[:, u * QB:(u + 1) * QB], 0.0) * iwt[IDX_DIM + h:IDX_DIM + h + 1, :]
        bits = pltpu.bitcast(sc, I32)
        skey = jnp.where(bits >= 0, bits, bits ^ 0x7FFFFFFF)
        if causal_mask:
            key_pos = row0 + lax.broadcasted_iota(I32, sc.shape, 0)
            q_pos = i * QB + lax.broadcasted_iota(I32, sc.shape, 1)
            skey = jnp.where(key_pos <= q_pos, skey, INT_MIN)
        sc_ref[pl.ds(row0, SCORE_CHUNK), :] = skey
        ukey = skey ^ INT_MIN
        for gi in range(SCORE_CHUNK // PLANE_KEYS):
            base = gi * PLANE_KEYS
            planes = _bit_transpose32([ukey[base + 8 * j:base + 8 * (j + 1), :] for j in range(32)])
            for b in range(32):
                plane_ref[c * (SCORE_CHUNK // PLANE_KEYS) + gi, b] = planes[b]

    def full_chunk(c, carry):
        score_chunk(c, False)
        return carry

    lax.fori_loop(0, n_chunks - 1, full_chunk, 0)
    score_chunk(n_chunks - 1, True)

    eq_ref[...] = jnp.full_like(eq_ref, -1)
    hit_ref[...] = jnp.full_like(hit_ref, -1)
    groups_per_iter = SCORE_CHUNK // PLANE_KEYS

    def settle(g, keep_prev):
        hit = hit_ref[g]
        return jnp.where(keep_prev != 0, hit, eq_ref[g] ^ hit)

    def bit_pass(it, carry):
        chosen, n_greater, keep_prev = carry
        bit = 31 - it

        def sweep(gg, acc):
            for u in range(groups_per_iter):
                g = gg * groups_per_iter + u
                eq = settle(g, keep_prev)
                hit = eq & plane_ref[g, bit]
                eq_ref[g] = eq
                hit_ref[g] = hit
                acc = acc + lax.population_count(hit)
            return acc

        acc = lax.fori_loop(0, n_chunks, sweep, jnp.zeros((8, QB), I32))
        n_ge = n_greater + jnp.sum(acc, axis=0, keepdims=True)
        keep = n_ge >= topk
        chosen = jnp.where(keep, chosen | (jnp.int32(1) << bit), chosen)
        n_greater = jnp.where(keep, n_greater, n_ge)
        return chosen, n_greater, keep.astype(I32)

    zeros_row = jnp.zeros((1, QB), I32)
    chosen, n_greater, keep_last = lax.fori_loop(0, 32, bit_pass, (zeros_row, zeros_row, zeros_row + 1))
    thr = jnp.maximum(chosen ^ INT_MIN, INT_MIN + 1)

    def count_equal(gg, acc):
        for u in range(groups_per_iter):
            acc = acc + lax.population_count(settle(gg * groups_per_iter + u, keep_last))
        return acc

    n_equal = jnp.sum(lax.fori_loop(0, n_chunks, count_equal, jnp.zeros((8, QB), I32)), axis=0, keepdims=True)
    n_ge_thr = jnp.where(chosen == 0, 0, n_greater + n_equal)

    n_count = n_chunks * (SCORE_CHUNK // COUNT_ROWS)

    def count_keys(pred):
        def body(r, acc):
            row0 = pl.multiple_of(r * COUNT_ROWS, COUNT_ROWS)
            hit = jnp.where(pred(sc_ref[pl.ds(row0, COUNT_ROWS), :], row0), 1, 0).astype(I32)
            return acc + jnp.sum(hit.reshape(COUNT_ROWS // 8, 8, QB), axis=0)
        acc = lax.fori_loop(0, n_count, body, jnp.zeros((8, QB), I32))
        return jnp.sum(acc, axis=0, keepdims=True)

    @pl.when(jnp.max(n_ge_thr) > topk)
    def _():
        need = topk - n_greater

        def key_pos_of(blk, row0):
            return row0 + lax.broadcasted_iota(I32, blk.shape, 0)

        def pos_step(it, last):
            cand = last | (jnp.int32(1) << (POS_BITS - 1 - it))
            n_before = count_keys(lambda blk, row0: jnp.logical_and(blk == thr, key_pos_of(blk, row0) < cand))
            return jnp.where(n_before < need, cand, last)

        last = lax.fori_loop(0, POS_BITS, pos_step, jnp.zeros((1, QB), I32))

        def demote(r, carry):
            row0 = pl.multiple_of(r * COUNT_ROWS, COUNT_ROWS)
            blk = sc_ref[pl.ds(row0, COUNT_ROWS), :]
            late_tie = jnp.logical_and(blk == thr, key_pos_of(blk, row0) > last)
            sc_ref[pl.ds(row0, COUNT_ROWS), :] = jnp.where(late_tie, thr - 1, blk)
            return carry

        lax.fori_loop(0, n_count, demote, 0)

    m_ref[...] = jnp.full_like(m_ref, M_INIT)
    acc_ref[...] = jnp.zeros_like(acc_ref)
    heads = range(A_KV_HEADS)

    def logits(c, slot):
        row0 = pl.multiple_of(c * KB, KB)
        sel = sc_ref[pl.ds(row0, KB), :] >= thr
        mask_cols = jnp.where(sel, 0.0, MASKED_LOGIT).astype(BF16)
        kblk = k_ref[pl.ds(row0, KB), :]
        for j in heads:
            kaug = jnp.concatenate([kblk[:, j * A_HEAD_DIM:(j + 1) * A_HEAD_DIM], mask_cols], axis=1)
            st_ref[slot, j] = jnp.dot(kaug, qt_ref[:, j * gq:(j + 1) * gq], preferred_element_type=F32)

    def softmax_pv(c, slot, near):
        ps, alphas = [], []
        for j in heads:
            st = st_ref[slot, j]
            if near != (None, None):
                halves = [st[0:QB], st[QB:KB]]
                halves = [hv if n is None else hv + bias_ref[n, j] for hv, n in zip(halves, near)]
                st = jnp.concatenate(halves, axis=0)
            m_old = m_ref[j]
            m_new = jnp.maximum(m_old, jnp.max(st, axis=0, keepdims=True))
            m_ref[j] = m_new
            ps.append(jnp.exp2(st - m_new).astype(BF16))
            alphas.append(jnp.exp2(m_old - m_new))
        vblk = vt_ref[c]
        pvs = [jnp.dot(vblk[j * VT_ROWS:(j + 1) * VT_ROWS, :], ps[j], preferred_element_type=F32)
               for j in heads]
        for j in heads:
            acc_ref[j] = acc_ref[j] * alphas[j] + pvs[j]

    far = (None, None)
    n_far = jnp.maximum(lax.shift_right_arithmetic(i - 1, 1), 0)
    i_odd = (i & 1) == 1

    @pl.when(i >= 1)
    def _():
        logits(0, 0)

    @pl.when(i == 0)
    def _():
        logits(0, 1)

    def far_pair(pp, carry):
        c0 = 2 * pp
        logits(c0 + 1, 1)
        softmax_pv(c0, 0, far)
        logits(c0 + 2, 0)
        softmax_pv(c0 + 1, 1, far)
        return carry

    lax.fori_loop(0, lax.shift_right_arithmetic(n_far, 1), far_pair, 0)

    @pl.when((n_far & 1) == 1)
    def _():
        softmax_pv(n_far - 1, 0, far)
        logits(n_far, 0)

    @pl.when(i_odd)
    def _():
        softmax_pv(n_far, 0, (1, 0))

    @pl.when(jnp.logical_and(jnp.logical_not(i_odd), i >= 2))
    def _():
        logits(n_far + 1, 1)
        softmax_pv(n_far, 0, (None, 1))

    @pl.when(jnp.logical_not(i_odd))
    def _():
        softmax_pv(lax.shift_right_arithmetic(i, 1), 1, (0, None))

    for j in heads:
        acc = acc_ref[j]
        out_t = acc[0:A_HEAD_DIM, :] * (1.0 / acc[A_HEAD_DIM:A_HEAD_DIM + 1, :])
        for g in range(A_GROUP):
            h = j * A_GROUP + g
            o_ref[:, h * A_HEAD_DIM:(h + 1) * A_HEAD_DIM] = out_t[:, g * QB:(g + 1) * QB].T.astype(o_ref.dtype)


def _dsa_attention(main, iw, vt3, bias, topk):
    s = main.shape[0]
    nb = s // QB
    q_blk = A_Q_COLS // A_Q_COLS
    del q_blk
    return pl.pallas_call(
        functools.partial(_attn_body, topk=topk),
        grid=(nb,),
        in_specs=[
            pl.BlockSpec((QB, A_Q_COLS), lambda i: (i, 0)),
            pl.BlockSpec((QB, A_IQ_COLS), lambda i: (i, (A_Q_COLS + 2 * A_KV_COLS) // A_IQ_COLS)),
            pl.BlockSpec((QB, LANES), lambda i: (i, 0)),
            _resident((s, A_KV_COLS), lambda i: (0, A_Q_COLS // A_KV_COLS)),
            _resident((s, LANES), lambda i: (0, A_MAIN_COLS // LANES)),
            _resident((s // KB, A_KV_HEADS * VT_ROWS, KB), lambda i: (0, 0, 0)),
            _resident((2, A_KV_HEADS, QB, A_GROUP * QB), lambda i: (0, 0, 0, 0)),
        ],
        out_specs=pl.BlockSpec((QB, A_Q_COLS), lambda i: (i, 0)),
        out_shape=jax.ShapeDtypeStruct((s, A_Q_COLS), BF16),
        scratch_shapes=[
            pltpu.VMEM((LANES, IDX_HEADS * QB), BF16),
            pltpu.VMEM((A_HEAD_DIM + QB, A_HEADS * QB), BF16),
            pltpu.VMEM((s, QB), I32),
            pltpu.VMEM((s // PLANE_KEYS, 32, 8, QB), I32),
            pltpu.VMEM((s // PLANE_KEYS, 8, QB), I32),
            pltpu.VMEM((s // PLANE_KEYS, 8, QB), I32),
            pltpu.VMEM((2, A_KV_HEADS, KB, A_GROUP * QB), F32),
            pltpu.VMEM((A_KV_HEADS, VT_ROWS, A_GROUP * QB), F32),
            pltpu.VMEM((A_KV_HEADS, 1, A_GROUP * QB), F32),
        ],
        compiler_params=_params("arbitrary"),
        name="dsa_attention",
    )(main, main, iw, main, main, vt3, bias)


def _dsa_mixer(x, gain, w_in, w_idx, w_out, layer, q_gain, k_gain, ik_gain_padded, bias):
    s = x.shape[0]
    topk = min(TOPK_MAX, s // 4)
    main, iw, vt3 = _dsa_in_proj(x, gain, w_in, w_idx, layer, q_gain, k_gain, ik_gain_padded)
    o = _dsa_attention(main, iw, vt3, bias, topk)
    return _out_proj(o, w_out, layer, x)


def _hgrn_sum_matrices():
    c = HG_CHUNK
    t = np.arange(c)[:, None]
    s = np.arange(c)[None, :]
    mats = []
    for l in range(1, HG_LEVELS + 1):
        blk = 1 << l
        m = (t // blk) * blk + blk // 2
        upper = t >= m
        mats.append(np.where(upper, (s >= m) & (s <= t), (s > t) & (s <= m - 1)))
    mats.append(s <= t)
    mats.append(s > t)
    return np.concatenate(mats, axis=0).astype(np.float32)


def _hgrn_split_levels():
    t = np.arange(HG_CHUNK)[:, None]
    s = np.arange(HG_CHUNK)[None, :]
    lvl = np.zeros((HG_CHUNK, HG_CHUNK), np.int32)
    for l in range(1, HG_LEVELS + 1):
        lvl[(s < t) & ((t >> l) == (s >> l)) & (((t >> (l - 1)) & 1) == 1) & (((s >> (l - 1)) & 1) == 0)] = l
    assert np.all((lvl > 0) == (s < t))
    return lvl


def _hgrn_body(q_ref, f_ref, i_ref, g_ref, lbl_ref, og_ref, gm_ref, split_ref, o_ref, st_ref, *, layer):
    c = pl.program_id(1)
    ch = HG_CHUNK

    @pl.when(c == 0)
    def _():
        st_ref[...] = jnp.zeros_like(st_ref)

    logits = lbl_ref[...]
    e = jnp.exp(logits - jnp.max(logits, axis=0, keepdims=True))
    p = e / jnp.sum(e, axis=0, keepdims=True)
    csum = p[0:1, :]
    for r in range(1, layer + 1):
        csum = csum + p[r:r + 1, :]
    lb = csum - p[0:1, :]

    def chunk(rows):
        fgate = lb + (1.0 - lb) * jax.nn.sigmoid(f_ref[rows, :])
        kin = 1.0 - fgate
        logf = jnp.log2(fgate)
        logf_hi = logf.astype(BF16)
        logf_lo = (logf - logf_hi.astype(F32)).astype(BF16)
        logf_split = jnp.concatenate([logf_hi, logf_lo], axis=0)
        decs = [jnp.dot(gm_ref[r * ch:(r + 1) * ch, :], logf_split, preferred_element_type=F32)
                for r in range(HG_LEVELS + 2)]

        row = lax.broadcasted_iota(I32, (ch, B_KEY_DIM), 0)
        split = split_ref[...]
        nt = (((1,), (1,)), ((), ()))
        heads = range(HG_HEADS_PER_STEP)
        levels = range(1, HG_LEVELS + 1)
        sls = [slice(hh * B_KEY_DIM, (hh + 1) * B_KEY_DIM) for hh in heads]

        qs = [q_ref[rows, sl] * (B_KEY_DIM ** -0.5) for sl in sls]
        ks = [kin[:, sl] for sl in sls]
        vs = [i_ref[rows, sl] for sl in sls]
        b_incl = [decs[HG_LEVELS][:, sl] for sl in sls]
        b_rest = [decs[HG_LEVELS + 1][:, sl] for sl in sls]
        states = [st_ref[hh] for hh in heads]

        zs = []
        for hh in heads:
            for l in levels:
                upper = (row & (1 << (l - 1))) != 0
                ex = jnp.exp2(decs[l - 1][:, sls[hh]])
                zs.append((jnp.where(upper, qs[hh], ks[hh]) * ex).astype(BF16))
        prods = [lax.dot_general(z, z, nt, preferred_element_type=F32) for z in zs]
        o_inter = [lax.dot_general((qs[hh] * jnp.exp2(b_incl[hh])).astype(BF16), states[hh].astype(BF16), nt,
                                   preferred_element_type=F32) for hh in heads]
        new_kv = [jnp.dot(vs[hh].T.astype(BF16), (ks[hh] * jnp.exp2(b_rest[hh])).astype(BF16),
                          preferred_element_type=F32) for hh in heads]
        a_mats = []
        for hh in heads:
            a = jnp.zeros((ch, ch), F32)
            for l in levels:
                a = jnp.where(split == l, prods[hh * HG_LEVELS + l - 1], a)
            a_mats.append(a.astype(BF16))
        o_intra = [jnp.dot(a_mats[hh], vs[hh].astype(BF16), preferred_element_type=F32) for hh in heads]

        for hh in heads:
            sl = sls[hh]
            st_ref[hh] = states[hh] * jnp.exp2(b_incl[hh][ch - 1:ch, :]) + new_kv[hh]
            o = o_inter[hh] + o_intra[hh] + jnp.sum(qs[hh] * ks[hh], axis=-1, keepdims=True) * vs[hh]
            on = o * _rms_scale(o) * og_ref[:, sl]
            gate = g_ref[rows, sl]
            o_ref[rows, sl] = (on * (gate * jax.nn.sigmoid(gate))).astype(o_ref.dtype)

    for cc in range(HG_CHUNKS_PER_STEP):
        chunk(slice(cc * ch, (cc + 1) * ch))


def _hgrn_core(proj, lb_logits, o_gain, layer):
    s = proj.shape[0]
    w = HG_HEADS_PER_STEP * B_KEY_DIM
    nhb = D_MODEL // w
    rows_per_step = HG_CHUNKS_PER_STEP * HG_CHUNK
    gm = jnp.asarray(np.tile(_hgrn_sum_matrices(), (1, 2)), dtype=BF16)
    split = jnp.asarray(_hgrn_split_levels())
    col = lambda off: (lambda hb, c: (c, off * nhb + hb))
    return pl.pallas_call(
        functools.partial(_hgrn_body, layer=layer),
        grid=(nhb, s // rows_per_step),
        in_specs=[
            pl.BlockSpec((rows_per_step, w), col(0)),
            pl.BlockSpec((rows_per_step, w), col(1)),
            pl.BlockSpec((rows_per_step, w), col(2)),
            pl.BlockSpec((rows_per_step, w), col(3)),
            pl.BlockSpec((DEPTH, w), lambda hb, c: (0, hb)),
            pl.BlockSpec((1, w), lambda hb, c: (0, hb)),
            pl.BlockSpec(gm.shape, lambda hb, c: (0, 0)),
            pl.BlockSpec(split.shape, lambda hb, c: (0, 0)),
        ],
        out_specs=pl.BlockSpec((rows_per_step, w), lambda hb, c: (c, hb)),
        out_shape=jax.ShapeDtypeStruct((s, D_MODEL), BF16),
        scratch_shapes=[pltpu.VMEM((HG_HEADS_PER_STEP, B_KEY_DIM, B_KEY_DIM), F32)],
        compiler_params=_params("parallel", "arbitrary"),
        name="hgrn_core",
    )(proj, proj, proj, proj, lb_logits, o_gain.reshape(1, D_MODEL), gm, split)


def _hgrn_mixer(x, gain, w_in, w_out, mixer_idx, lb_logits, o_gain, layer):
    proj = _hgrn_in_proj(x, gain, w_in, mixer_idx)
    o = _hgrn_core(proj, lb_logits, o_gain, layer)
    return _out_proj(o, w_out, mixer_idx, x)


def kernel(x, norm_gains, ffn_w_gate, ffn_w_up, ffn_w_down, dsa_w_in, dsa_w_out, dsa_q_gain, dsa_k_gain,
           dsa_idx_k_gain, rel_bias, hgrn_w_in, hgrn_w_out, hgrn_lb_logits, hgrn_o_gain):
    assert x.shape[0] == 1 and x.shape[2] == D_MODEL and x.shape[1] % FFN_TM == 0
    h = x.reshape(x.shape[1], D_MODEL)
    dsa_w_idx = jnp.pad(dsa_w_in[:, :, A_MAIN_COLS:], ((0, 0), (0, 0), (0, LANES - (A_IN_COLS - A_MAIN_COLS))))
    ik_gain = jnp.pad(dsa_idx_k_gain, ((0, 0), (0, LANES - IDX_DIM)))
    bias = _near_bias(rel_bias)

    for layer in range(DEPTH):
        j = layer // N_MIXERS
        h = _ffn(h, norm_gains[layer, 0], ffn_w_gate, ffn_w_up, ffn_w_down, (layer, 0))
        if layer % N_MIXERS == 0:
            h = _dsa_mixer(h, norm_gains[layer, 1], dsa_w_in, dsa_w_idx, dsa_w_out, j, dsa_q_gain[j], dsa_k_gain[j],
                           ik_gain[j:j + 1], bias)
        else:
            h = _hgrn_mixer(h, norm_gains[layer, 1], hgrn_w_in, hgrn_w_out, j, hgrn_lb_logits, hgrn_o_gain[j], layer)
        h = _ffn(h, norm_gains[layer, 2], ffn_w_gate, ffn_w_up, ffn_w_down, (layer, 1))
    return h.reshape(x.shape)
```

```python
import functools
import math

import numpy as np
import jax
import jax.numpy as jnp
from jax import lax
from jax.experimental import pallas as pl
from jax.experimental.pallas import tpu as pltpu

F32 = jnp.float32
BF16 = jnp.bfloat16
I32 = jnp.int32

D_MODEL = 2048
DEPTH = 4
N_MIXERS = 2
A_HEADS = 16
A_HEAD_DIM = 128
A_KV_HEADS = 4
A_GROUP = A_HEADS // A_KV_HEADS
IDX_HEADS = 16
IDX_DIM = 64
TOPK_MAX = 256
REL_BUCKETS = 32
REL_MAX_DIST = 128
B_KEY_DIM = 128
B_HEADS = D_MODEL // B_KEY_DIM
D_FF = 5632
EPS = 1e-6

A_Q_COLS = A_HEADS * A_HEAD_DIM
A_KV_COLS = A_KV_HEADS * A_HEAD_DIM
A_IQ_COLS = IDX_HEADS * IDX_DIM
A_MAIN_COLS = A_Q_COLS + 2 * A_KV_COLS + A_IQ_COLS
A_IN_COLS = A_MAIN_COLS + IDX_DIM + IDX_HEADS

LANES = 128
V7X_VMEM_BYTES = 64 * 1024 * 1024
VMEM_LIMIT_BYTES = 56 * 1024 * 1024
FFN_REST_VMEM_LIMIT_BYTES = 58 * 1024 * 1024

INT_MIN = -(2 ** 31)
LOG2_E = math.log2(math.e)
MASKED_LOGIT = -1e30
M_INIT = -1e20
VT_ROWS = A_HEAD_DIM + 16

FFN_TM = 1024
FFN_TF = 512
FFN_TF_FIRST = 256
FFN_COPY_ROWS = 128
NMM_TM = 1024
NMM_TN = 512
HG_IN_TN = 1024
NMM_ROWS = 256
OUT_TM = 512
QB = 128
KB = 2 * QB
SCORE_CHUNK = 512
KEY_BITS = 32
SUBLANES = 8
PLANE_KEYS = KEY_BITS * SUBLANES
COUNT_ROWS = 512
HG_CHUNK = 128
HG_HEADS_PER_STEP = 4
HG_CHUNKS_PER_STEP = 4
HG_LEVELS = 7


def _params(*semantics, vmem_limit_bytes=VMEM_LIMIT_BYTES):
    return pltpu.CompilerParams(dimension_semantics=semantics, vmem_limit_bytes=vmem_limit_bytes)


def _resident(shape, index_map):
    return pl.BlockSpec(shape, index_map, pipeline_mode=pl.Buffered(1))


def _rms_scale(x):
    return lax.rsqrt(jnp.mean(x * x, axis=-1, keepdims=True) + EPS)


def _ffn_step(j, n_steps, x_ref, g_ref, weights, o_ref, h_ref):
    @pl.when(j == 0)
    def _():
        x = x_ref[...]
        h_ref[...] = (x * _rms_scale(x) * g_ref[...]).astype(BF16)
        o_ref[...] = jnp.zeros_like(o_ref)

    h = h_ref[...]
    wg, wu, wd = weights()
    gate = jnp.dot(h, wg, preferred_element_type=F32)
    up = jnp.dot(h, wu, preferred_element_type=F32)
    act = (gate * jax.nn.sigmoid(gate) * up).astype(BF16)
    o_ref[...] += jnp.dot(act, wd, preferred_element_type=F32)

    @pl.when(j == n_steps - 1)
    def _():
        o_ref[...] = x_ref[...] + 0.5 * o_ref[...]


def _ffn_first_body(x_ref, g_ref, wg_ref, wu_ref, wd_ref, y_ref, wgb_ref, wub_ref, wdb_ref, h_ref):
    def weights():
        tiles = []
        for src, dst in ((wg_ref, wgb_ref), (wu_ref, wub_ref), (wd_ref, wdb_ref)):
            tiles.append(src[...].astype(BF16))
            dst[...] = tiles[-1]
        return tiles

    _ffn_step(pl.program_id(0), pl.num_programs(0), x_ref, g_ref, weights, y_ref, h_ref)


def _ffn_rest_body(x_ref, y0_ref, g_ref, wg_ref, wu_ref, wd_ref, o_ref, h_ref):
    i = pl.program_id(0)
    j = pl.program_id(1)

    @pl.when(jnp.logical_and(i == 0, j < o_ref.shape[0] // FFN_COPY_ROWS))
    def _():
        o_ref[pl.ds(pl.multiple_of(j * FFN_COPY_ROWS, FFN_COPY_ROWS), FFN_COPY_ROWS), :] = y0_ref[...]

    @pl.when(i > 0)
    def _():
        _ffn_step(j, pl.num_programs(1), x_ref, g_ref, lambda: (wg_ref[...], wu_ref[...], wd_ref[...]), o_ref, h_ref)


def _ffn(x, gain, wg32, wu32, wd32, lead):
    s, d = x.shape
    f = wg32.shape[-1]
    tm = min(FFN_TM, s)
    squeeze = (None,) * len(lead)
    n_copy = tm // FFN_COPY_ROWS
    assert f // FFN_TF >= n_copy
    y0, wg, wu, wd = pl.pallas_call(
        _ffn_first_body,
        grid=(f // FFN_TF_FIRST,),
        in_specs=[
            _resident((tm, d), lambda j: (0, 0)),
            pl.BlockSpec((1, d), lambda j: (0, 0)),
            pl.BlockSpec(squeeze + (d, FFN_TF_FIRST), lambda j: lead + (0, j)),
            pl.BlockSpec(squeeze + (d, FFN_TF_FIRST), lambda j: lead + (0, j)),
            pl.BlockSpec(squeeze + (FFN_TF_FIRST, d), lambda j: lead + (j, 0)),
        ],
        out_specs=(
            pl.BlockSpec((tm, d), lambda j: (0, 0)),
            pl.BlockSpec((d, FFN_TF_FIRST), lambda j: (0, j)),
            pl.BlockSpec((d, FFN_TF_FIRST), lambda j: (0, j)),
            pl.BlockSpec((FFN_TF_FIRST, d), lambda j: (j, 0)),
        ),
        out_shape=(jax.ShapeDtypeStruct((tm, d), F32), jax.ShapeDtypeStruct((d, f), BF16),
                   jax.ShapeDtypeStruct((d, f), BF16), jax.ShapeDtypeStruct((f, d), BF16)),
        scratch_shapes=[pltpu.VMEM((tm, d), BF16)],
        compiler_params=_params("arbitrary"),
        name="ffn_first",
    )(x, gain.reshape(1, d), wg32, wu32, wd32)
    wcol = lambda i, j: (0, jnp.where(i > 0, j, 0))
    wrow = lambda i, j: (jnp.where(i > 0, j, 0), 0)
    return pl.pallas_call(
        _ffn_rest_body,
        grid=(s // tm, f // FFN_TF),
        in_specs=[
            pl.BlockSpec((tm, d), lambda i, j: (jnp.minimum(jnp.maximum(i, 1), s // tm - 1), 0)),
            pl.BlockSpec((FFN_COPY_ROWS, d), lambda i, j: (jnp.where(i == 0, jnp.minimum(j, n_copy - 1), n_copy - 1), 0)),
            pl.BlockSpec((1, d), lambda i, j: (0, 0)),
            pl.BlockSpec((d, FFN_TF), wcol),
            pl.BlockSpec((d, FFN_TF), wcol),
            pl.BlockSpec((FFN_TF, d), wrow),
        ],
        out_specs=pl.BlockSpec((tm, d), lambda i, j: (i, 0)),
        out_shape=jax.ShapeDtypeStruct((s, d), F32),
        scratch_shapes=[pltpu.VMEM((tm, d), BF16)],
        compiler_params=_params("arbitrary", "arbitrary", vmem_limit_bytes=FFN_REST_VMEM_LIMIT_BYTES),
        name="ffn_rest",
    )(x, y0, gain.reshape(1, d), wg, wu, wd)


def _nmm_body(*refs, n_extra, epilogue):
    x_ref, g_ref, w_ref = refs[:3]
    extra = refs[3:3 + n_extra]
    out_refs = refs[3 + n_extra:-1]
    h_ref = refs[-1]
    n = pl.program_id(1)

    @pl.when(n == 0)
    def _():
        x = x_ref[...]
        h_ref[...] = (x * _rms_scale(x) * g_ref[...]).astype(BF16)

    epilogue(n, h_ref, w_ref, extra, out_refs)


assert NMM_ROWS == KB


def _row_chunks(h_ref, w_ref, emit):
    w = w_ref[...].astype(BF16)
    for r in range(h_ref.shape[0] // NMM_ROWS):
        rows = slice(r * NMM_ROWS, (r + 1) * NMM_ROWS)
        emit(rows, jnp.dot(h_ref[rows, :], w, preferred_element_type=F32))


def _norm_matmul(x, gain, w, layer, n_tiles, extra, extra_specs, out_shapes, out_specs, epilogue, tn=NMM_TN):
    s, d = x.shape
    tm = min(NMM_TM, s)
    last_whole = w.shape[-1] // tn - 1
    return pl.pallas_call(
        functools.partial(_nmm_body, n_extra=len(extra), epilogue=epilogue),
        grid=(s // tm, n_tiles),
        in_specs=[
            pl.BlockSpec((tm, d), lambda i, n: (i, 0)),
            pl.BlockSpec((1, d), lambda i, n: (0, 0)),
            pl.BlockSpec((None, d, tn), lambda i, n: (layer, 0, jnp.minimum(n, last_whole))),
        ] + list(extra_specs),
        out_specs=out_specs,
        out_shape=out_shapes,
        scratch_shapes=[pltpu.VMEM((tm, d), BF16)],
        compiler_params=_params("parallel", "arbitrary"),
        name="norm_matmul",
    )(x, gain.reshape(1, d), w, *extra)


def _head_norm(y, gain, scale):
    parts = []
    for c in range(NMM_TN // A_HEAD_DIM):
        z = y[:, c * A_HEAD_DIM:(c + 1) * A_HEAD_DIM]
        parts.append(z * _rms_scale(z) * (gain * scale))
    return jnp.concatenate(parts, axis=1)


def _dsa_in_epilogue(n, h_ref, w_ref, extra, out_refs):
    qg_ref, kg_ref, ikg_ref, widx_ref = extra
    main_ref, iw_ref, vt_ref = out_refs
    q_tiles = A_Q_COLS // NMM_TN
    k_tile = q_tiles
    v_tile = k_tile + 1
    iq_tile0 = v_tile + 1
    idx_tile = A_MAIN_COLS // NMM_TN

    def store_main(fn):
        def emit(rows, y):
            main_ref[rows, :] = fn(y).astype(BF16)
        _row_chunks(h_ref, w_ref, emit)

    @pl.when(n < q_tiles)
    def _():
        store_main(lambda y: _head_norm(y, qg_ref[...], A_HEAD_DIM ** -0.5 * LOG2_E))

    @pl.when(n == k_tile)
    def _():
        store_main(lambda y: _head_norm(y, kg_ref[...], 1.0))

    @pl.when(n == v_tile)
    def _():
        tail = (lax.broadcasted_iota(I32, (VT_ROWS - A_HEAD_DIM, KB), 0) == 0).astype(BF16)

        def emit(rows, y):
            main_ref[rows, :] = y.astype(BF16)
            blk = rows.start // KB
            for j in range(A_KV_HEADS):
                vt_ref[blk, j * VT_ROWS:j * VT_ROWS + A_HEAD_DIM, :] = (
                    y[:, j * A_HEAD_DIM:(j + 1) * A_HEAD_DIM].T.astype(BF16))
                vt_ref[blk, j * VT_ROWS + A_HEAD_DIM:(j + 1) * VT_ROWS, :] = tail
        _row_chunks(h_ref, w_ref, emit)

    @pl.when(jnp.logical_and(n >= iq_tile0, n < idx_tile))
    def _():
        store_main(lambda y: y * IDX_DIM ** -0.5)

    @pl.when(n == idx_tile)
    def _():
        def emit(rows, z):
            lane = lax.broadcasted_iota(I32, z.shape, 1)
            ms = jnp.sum(jnp.where(lane < IDX_DIM, z * z, 0.0), axis=-1, keepdims=True) * (1.0 / IDX_DIM)
            ikn = z * lax.rsqrt(ms + EPS) * ikg_ref[...]
            main_ref[rows, :] = jnp.concatenate(
                [ikn, jnp.zeros((z.shape[0], NMM_TN - LANES), F32)], axis=1).astype(BF16)
            iw_ref[rows, :] = z * IDX_HEADS ** -0.5
        _row_chunks(h_ref, widx_ref, emit)


def _dsa_in_proj(x, gain, w_in, w_idx, layer, q_gain, k_gain, ik_gain_padded):
    s, d = x.shape
    tm = min(NMM_TM, s)
    n_tiles = A_MAIN_COLS // NMM_TN + 1
    ncols = n_tiles * NMM_TN
    vec = lambda i, n: (0, 0)
    return _norm_matmul(
        x, gain, w_in, layer, n_tiles,
        extra=(q_gain.reshape(1, A_HEAD_DIM), k_gain.reshape(1, A_HEAD_DIM), ik_gain_padded, w_idx),
        extra_specs=[pl.BlockSpec((1, A_HEAD_DIM), vec), pl.BlockSpec((1, A_HEAD_DIM), vec),
                     pl.BlockSpec((1, LANES), vec), pl.BlockSpec((None, d, LANES), lambda i, n: (layer, 0, 0))],
        out_shapes=(jax.ShapeDtypeStruct((s, ncols), BF16), jax.ShapeDtypeStruct((s, LANES), F32),
                    jax.ShapeDtypeStruct((s // KB, A_KV_HEADS * VT_ROWS, KB), BF16)),
        out_specs=(pl.BlockSpec((tm, NMM_TN), lambda i, n: (i, n)),
                   pl.BlockSpec((tm, LANES), lambda i, n: (i, 0)),
                   pl.BlockSpec((tm // KB, A_KV_HEADS * VT_ROWS, KB), lambda i, n: (i, 0, 0))),
        epilogue=_dsa_in_epilogue,
    )


def _identity_epilogue(n, h_ref, w_ref, extra, out_refs):
    def emit(rows, y):
        out_refs[0][rows, :] = y
    _row_chunks(h_ref, w_ref, emit)


def _hgrn_in_proj(x, gain, w, layer):
    s = x.shape[0]
    tm = min(NMM_TM, s)
    return _norm_matmul(
        x, gain, w, layer, w.shape[-1] // HG_IN_TN, extra=(), extra_specs=[],
        out_shapes=jax.ShapeDtypeStruct((s, w.shape[-1]), F32),
        out_specs=pl.BlockSpec((tm, HG_IN_TN), lambda i, n: (i, n)),
        epilogue=_identity_epilogue, tn=HG_IN_TN,
    )


def _out_proj_body(a_ref, w_ref, r_ref, o_ref, wb_ref):
    @pl.when(pl.program_id(0) == 0)
    def _():
        wb_ref[...] = w_ref[...].astype(BF16)

    o_ref[...] = r_ref[...] + jnp.dot(a_ref[...], wb_ref[...], preferred_element_type=F32)


def _out_proj(a, w, layer, res):
    s, k = a.shape
    d = w.shape[-1]
    tm = min(OUT_TM, s)
    return pl.pallas_call(
        _out_proj_body,
        grid=(s // tm,),
        in_specs=[
            pl.BlockSpec((tm, k), lambda i: (i, 0)),
            _resident((None, k, d), lambda i: (layer, 0, 0)),
            pl.BlockSpec((tm, d), lambda i: (i, 0)),
        ],
        out_specs=pl.BlockSpec((tm, d), lambda i: (i, 0)),
        out_shape=jax.ShapeDtypeStruct((s, d), F32),
        scratch_shapes=[pltpu.VMEM((k, d), BF16)],
        compiler_params=_params("arbitrary"),
        name="out_proj",
    )(a, w, res)


def _t5_bucket_table():
    exact = REL_BUCKETS // 2
    d = np.arange(2 * QB)
    df = np.maximum(d, 1).astype(np.float64)
    val = np.log(df / exact) / math.log(REL_MAX_DIST / exact) * (REL_BUCKETS - exact)
    big = d >= exact
    frac = np.abs(val[big] - np.round(val[big]))
    assert np.all((frac > 1e-4) | (frac == 0.0))
    large = np.minimum(exact + np.floor(val + 1e-9).astype(np.int64), REL_BUCKETS - 1)
    return np.where(d < exact, d, large).astype(np.int32)


def _near_bucket_tiles():
    table = _t5_bucket_table()
    r = np.arange(QB)[:, None]
    t = np.arange(QB)[None, :]
    tiles = [table[np.maximum(n * QB + t - r, 0)] for n in range(2)]
    return np.stack(tiles).astype(np.int32)


def _bias_body(rel_ref, bkt_ref, o_ref):
    j = pl.program_id(1)
    bkt = bkt_ref[0]
    for g in range(A_GROUP):
        h = j * A_GROUP + g
        far = rel_ref[REL_BUCKETS - 1, h]
        acc = jnp.zeros(bkt.shape, F32)
        for b in range(REL_BUCKETS):
            acc = jnp.where(bkt == b, (rel_ref[b, h] - far) * LOG2_E, acc)
        o_ref[0, 0, :, g * QB:(g + 1) * QB] = acc


def _near_bias(rel_bias):
    bkt = jnp.asarray(_near_bucket_tiles())
    return pl.pallas_call(
        _bias_body,
        grid=(2, A_KV_HEADS),
        in_specs=[
            pl.BlockSpec(memory_space=pltpu.SMEM),
            pl.BlockSpec((1, QB, QB), lambda n, j: (n, 0, 0)),
        ],
        out_specs=pl.BlockSpec((1, 1, QB, A_GROUP * QB), lambda n, j: (n, j, 0, 0)),
        out_shape=jax.ShapeDtypeStruct((2, A_KV_HEADS, QB, A_GROUP * QB), F32),
        compiler_params=_params("arbitrary", "arbitrary"),
        name="near_bias",
    )(rel_bias, bkt)


def _bit_transpose32(words):
    a = list(words)
    for j, mask in ((16, 0x0000FFFF), (8, 0x00FF00FF), (4, 0x0F0F0F0F), (2, 0x33333333), (1, 0x55555555)):
        shift = jnp.full(a[0].shape, j, I32)
        for k in range(32):
            if k & j == 0:
                t = (lax.shift_right_logical(a[k], shift) ^ a[k + j]) & mask
                a[k + j] = a[k + j] ^ t
                a[k] = a[k] ^ lax.shift_left(t, shift)
    return a


def _attn_body(q_ref, iq_ref, iw_ref, k_ref, ik_ref, vt_ref, bias_ref, o_ref,
               iqt_ref, qt_ref, sc_ref, plane_ref, eq_ref, hit_ref, st_ref, bmax_ref, acc_ref, m_ref, *, topk):
    i = pl.program_id(0)
    gq = A_GROUP * QB
    POS_BITS = (sc_ref.shape[0] - 1).bit_length()

    iqt = iq_ref[...].astype(F32).T
    iqt_ref[...] = jnp.zeros_like(iqt_ref)
    for h in range(IDX_HEADS):
        iqt_ref[0:IDX_DIM, h * QB:(h + 1) * QB] = iqt[h * IDX_DIM:(h + 1) * IDX_DIM, :].astype(BF16)
    qt = q_ref[...].astype(F32).T
    eye = (lax.broadcasted_iota(I32, (QB, QB), 0) == lax.broadcasted_iota(I32, (QB, QB), 1)).astype(BF16)
    for h in range(A_HEADS):
        qt_ref[0:A_HEAD_DIM, h * QB:(h + 1) * QB] = qt[h * A_HEAD_DIM:(h + 1) * A_HEAD_DIM, :].astype(BF16)
        qt_ref[A_HEAD_DIM:, h * QB:(h + 1) * QB] = eye
    iwt = iw_ref[...].T

    n_chunks = (i * QB + QB + SCORE_CHUNK - 1) // SCORE_CHUNK

    def score_chunk(c, causal_mask):
        row0 = pl.multiple_of(c * SCORE_CHUNK, SCORE_CHUNK)
        ikc = ik_ref[pl.ds(row0, SCORE_CHUNK), :]
        sc = jnp.zeros((SCORE_CHUNK, QB), F32)
        for hp in range(IDX_HEADS // 2):
            raw = jnp.dot(ikc, iqt_ref[:, hp * 2 * QB:(hp + 1) * 2 * QB], preferred_element_type=F32)
            for u in range(2):
                h = 2 * hp + u
                sc = sc + jnp.maximum(raw[:, u * QB:(u + 1) * QB], 0.0) * iwt[IDX_DIM + h:IDX_DIM + h + 1, :]
        bits = pltpu.bitcast(sc, I32)
        skey = jnp.where(bits >= 0, bits, bits ^ 0x7FFFFFFF)
        if causal_mask:
            key_pos = row0 + lax.broadcasted_iota(I32, sc.shape, 0)
            q_pos = i * QB + lax.broadcasted_iota(I32, sc.shape, 1)
            skey = jnp.where(key_pos <= q_pos, skey, INT_MIN)
        sc_ref[pl.ds(row0, SCORE_CHUNK), :] = skey
        ukey = skey ^ INT_MIN
        for gi in range(SCORE_CHUNK // PLANE_KEYS):
            base = gi * PLANE_KEYS
            planes = _bit_transpose32([ukey[base + SUBLANES * j:base + SUBLANES * (j + 1), :]
                                       for j in range(KEY_BITS)])
            for b in range(KEY_BITS):
                plane_ref[c * (SCORE_CHUNK // PLANE_KEYS) + gi, b] = planes[b]

    def full_chunk(c, carry):
        score_chunk(c, False)
        return carry

    lax.fori_loop(0, n_chunks - 1, full_chunk, 0)
    score_chunk(n_chunks - 1, True)

    eq_ref[...] = jnp.full_like(eq_ref, -1)
    hit_ref[...] = jnp.full_like(hit_ref, -1)
    groups_per_iter = SCORE_CHUNK // PLANE_KEYS

    def settle(g, keep_prev):
        hit = hit_ref[g]
        return jnp.where(keep_prev != 0, hit, eq_ref[g] ^ hit)

    def bit_pass(it, carry):
        chosen, n_greater, keep_prev = carry
        bit = KEY_BITS - 1 - it

        def sweep(gg, acc):
            for u in range(groups_per_iter):
                g = gg * groups_per_iter + u
                eq = settle(g, keep_prev)
                hit = eq & plane_ref[g, bit]
                eq_ref[g] = eq
                hit_ref[g] = hit
                acc = acc + lax.population_count(hit)
            return acc

        acc = lax.fori_loop(0, n_chunks, sweep, jnp.zeros((SUBLANES, QB), I32))
        n_ge = n_greater + jnp.sum(acc, axis=0, keepdims=True)
        keep = n_ge >= topk
        chosen = jnp.where(keep, chosen | (jnp.int32(1) << bit), chosen)
        n_greater = jnp.where(keep, n_greater, n_ge)
        return chosen, n_greater, keep.astype(I32)

    zeros_row = jnp.zeros((1, QB), I32)
    chosen, n_greater, keep_last = lax.fori_loop(0, KEY_BITS, bit_pass, (zeros_row, zeros_row, zeros_row + 1))
    thr = jnp.maximum(chosen ^ INT_MIN, INT_MIN + 1)

    def count_equal(gg, acc):
        for u in range(groups_per_iter):
            acc = acc + lax.population_count(settle(gg * groups_per_iter + u, keep_last))
        return acc

    n_equal = jnp.sum(lax.fori_loop(0, n_chunks, count_equal, jnp.zeros((SUBLANES, QB), I32)), axis=0, keepdims=True)
    n_ge_thr = jnp.where(chosen == 0, 0, n_greater + n_equal)

    n_count = n_chunks * (SCORE_CHUNK // COUNT_ROWS)

    def count_keys(pred):
        def body(r, acc):
            row0 = pl.multiple_of(r * COUNT_ROWS, COUNT_ROWS)
            hit = jnp.where(pred(sc_ref[pl.ds(row0, COUNT_ROWS), :], row0), 1, 0).astype(I32)
            return acc + jnp.sum(hit.reshape(COUNT_ROWS // SUBLANES, SUBLANES, QB), axis=0)
        acc = lax.fori_loop(0, n_count, body, jnp.zeros((SUBLANES, QB), I32))
        return jnp.sum(acc, axis=0, keepdims=True)

    @pl.when(jnp.max(n_ge_thr) > topk)
    def _():
        need = topk - n_greater

        def key_pos_of(blk, row0):
            return row0 + lax.broadcasted_iota(I32, blk.shape, 0)

        def pos_step(it, last):
            cand = last | (jnp.int32(1) << (POS_BITS - 1 - it))
            n_before = count_keys(lambda blk, row0: jnp.logical_and(blk == thr, key_pos_of(blk, row0) < cand))
            return jnp.where(n_before < need, cand, last)

        last = lax.fori_loop(0, POS_BITS, pos_step, jnp.zeros((1, QB), I32))

        def demote(r, carry):
            row0 = pl.multiple_of(r * COUNT_ROWS, COUNT_ROWS)
            blk = sc_ref[pl.ds(row0, COUNT_ROWS), :]
            late_tie = jnp.logical_and(blk == thr, key_pos_of(blk, row0) > last)
            sc_ref[pl.ds(row0, COUNT_ROWS), :] = jnp.where(late_tie, thr - 1, blk)
            return carry

        lax.fori_loop(0, n_count, demote, 0)

    m_ref[...] = jnp.full_like(m_ref, M_INIT)
    acc_ref[...] = jnp.zeros_like(acc_ref)
    heads = range(A_KV_HEADS)

    def logits(c, slot):
        row0 = pl.multiple_of(c * KB, KB)
        sel = sc_ref[pl.ds(row0, KB), :] >= thr
        mask_cols = jnp.where(sel, 0.0, MASKED_LOGIT).astype(BF16)
        kblk = k_ref[pl.ds(row0, KB), :]
        for j in heads:
            kaug = jnp.concatenate([kblk[:, j * A_HEAD_DIM:(j + 1) * A_HEAD_DIM], mask_cols], axis=1)
            st = jnp.dot(kaug, qt_ref[:, j * gq:(j + 1) * gq], preferred_element_type=F32)
            st_ref[slot, j] = st
            bmax_ref[slot, j] = jnp.max(st, axis=0, keepdims=True)

    def softmax_pv(c, slot, near):
        ps, alphas = [], []
        for j in heads:
            st = st_ref[slot, j]
            if near != (None, None):
                halves = [st[0:QB], st[QB:KB]]
                halves = [hv if n is None else hv + bias_ref[n, j] for hv, n in zip(halves, near)]
                st = jnp.concatenate(halves, axis=0)
                block_max = jnp.max(st, axis=0, keepdims=True)
            else:
                block_max = bmax_ref[slot, j]
            m_old = m_ref[j]
            m_new = jnp.maximum(m_old, block_max)
            m_ref[j] = m_new
            ps.append(jnp.exp2(st - m_new).astype(BF16))
            alphas.append(jnp.exp2(m_old - m_new))
        vblk = vt_ref[c]
        pvs = [jnp.dot(vblk[j * VT_ROWS:(j + 1) * VT_ROWS, :], ps[j], preferred_element_type=F32)
               for j in heads]
        for j in heads:
            acc_ref[j] = acc_ref[j] * alphas[j] + pvs[j]

    far = (None, None)
    n_far = jnp.maximum(lax.shift_right_arithmetic(i - 1, 1), 0)
    i_odd = (i & 1) == 1

    @pl.when(i >= 1)
    def _():
        logits(0, 0)

    @pl.when(i == 0)
    def _():
        logits(0, 1)

    def far_pair(pp, carry):
        c0 = 2 * pp
        logits(c0 + 1, 1)
        softmax_pv(c0, 0, far)
        logits(c0 + 2, 0)
        softmax_pv(c0 + 1, 1, far)
        return carry

    lax.fori_loop(0, lax.shift_right_arithmetic(n_far, 1), far_pair, 0)

    @pl.when((n_far & 1) == 1)
    def _():
        softmax_pv(n_far - 1, 0, far)
        logits(n_far, 0)

    @pl.when(i_odd)
    def _():
        softmax_pv(n_far, 0, (1, 0))

    @pl.when(jnp.logical_and(jnp.logical_not(i_odd), i >= 2))
    def _():
        logits(n_far + 1, 1)
        softmax_pv(n_far, 0, (None, 1))

    @pl.when(jnp.logical_not(i_odd))
    def _():
        softmax_pv(lax.shift_right_arithmetic(i, 1), 1, (0, None))

    for j in heads:
        acc = acc_ref[j]
        out_t = acc[0:A_HEAD_DIM, :] * (1.0 / acc[A_HEAD_DIM:A_HEAD_DIM + 1, :])
        for g in range(A_GROUP):
            h = j * A_GROUP + g
            o_ref[:, h * A_HEAD_DIM:(h + 1) * A_HEAD_DIM] = out_t[:, g * QB:(g + 1) * QB].T.astype(o_ref.dtype)


def _dsa_attention(main, iw, vt3, bias, topk):
    s = main.shape[0]
    nb = s // QB
    return pl.pallas_call(
        functools.partial(_attn_body, topk=topk),
        grid=(nb,),
        in_specs=[
            pl.BlockSpec((QB, A_Q_COLS), lambda i: (i, 0)),
            pl.BlockSpec((QB, A_IQ_COLS), lambda i: (i, (A_Q_COLS + 2 * A_KV_COLS) // A_IQ_COLS)),
            pl.BlockSpec((QB, LANES), lambda i: (i, 0)),
            _resident((s, A_KV_COLS), lambda i: (0, A_Q_COLS // A_KV_COLS)),
            _resident((s, LANES), lambda i: (0, A_MAIN_COLS // LANES)),
            _resident((s // KB, A_KV_HEADS * VT_ROWS, KB), lambda i: (0, 0, 0)),
            _resident((2, A_KV_HEADS, QB, A_GROUP * QB), lambda i: (0, 0, 0, 0)),
        ],
        out_specs=pl.BlockSpec((QB, A_Q_COLS), lambda i: (i, 0)),
        out_shape=jax.ShapeDtypeStruct((s, A_Q_COLS), BF16),
        scratch_shapes=[
            pltpu.VMEM((LANES, IDX_HEADS * QB), BF16),
            pltpu.VMEM((A_HEAD_DIM + QB, A_HEADS * QB), BF16),
            pltpu.VMEM((s, QB), I32),
            pltpu.VMEM((s // PLANE_KEYS, KEY_BITS, SUBLANES, QB), I32),
            pltpu.VMEM((s // PLANE_KEYS, SUBLANES, QB), I32),
            pltpu.VMEM((s // PLANE_KEYS, SUBLANES, QB), I32),
            pltpu.VMEM((2, A_KV_HEADS, KB, A_GROUP * QB), F32),
            pltpu.VMEM((2, A_KV_HEADS, 1, A_GROUP * QB), F32),
            pltpu.VMEM((A_KV_HEADS, VT_ROWS, A_GROUP * QB), F32),
            pltpu.VMEM((A_KV_HEADS, 1, A_GROUP * QB), F32),
        ],
        compiler_params=_params("arbitrary"),
        name="dsa_attention",
    )(main, main, iw, main, main, vt3, bias)


def _dsa_mixer(x, gain, w_in, w_idx, w_out, layer, q_gain, k_gain, ik_gain_padded, bias):
    s = x.shape[0]
    topk = min(TOPK_MAX, s // 4)
    main, iw, vt3 = _dsa_in_proj(x, gain, w_in, w_idx, layer, q_gain, k_gain, ik_gain_padded)
    o = _dsa_attention(main, iw, vt3, bias, topk)
    return _out_proj(o, w_out, layer, x)


def _hgrn_sum_matrices():
    c = HG_CHUNK
    t = np.arange(c)[:, None]
    s = np.arange(c)[None, :]
    mats = []
    for l in range(1, HG_LEVELS + 1):
        blk = 1 << l
        m = (t // blk) * blk + blk // 2
        upper = t >= m
        mats.append(np.where(upper, (s >= m) & (s <= t), (s > t) & (s <= m - 1)))
    mats.append(s <= t)
    mats.append(s > t)
    return np.concatenate(mats, axis=0).astype(np.float32)


def _hgrn_split_levels():
    t = np.arange(HG_CHUNK)[:, None]
    s = np.arange(HG_CHUNK)[None, :]
    lvl = np.zeros((HG_CHUNK, HG_CHUNK), np.int32)
    for l in range(1, HG_LEVELS + 1):
        lvl[(s < t) & ((t >> l) == (s >> l)) & (((t >> (l - 1)) & 1) == 1) & (((s >> (l - 1)) & 1) == 0)] = l
    assert np.all((lvl > 0) == (s < t))
    return lvl


def _hgrn_body(q_ref, f_ref, i_ref, g_ref, lbl_ref, og_ref, gm_ref, split_ref, o_ref, st_ref, *, layer):
    c = pl.program_id(1)
    ch = HG_CHUNK

    @pl.when(c == 0)
    def _():
        st_ref[...] = jnp.zeros_like(st_ref)

    logits = lbl_ref[...]
    e = jnp.exp(logits - jnp.max(logits, axis=0, keepdims=True))
    p = e / jnp.sum(e, axis=0, keepdims=True)
    csum = p[0:1, :]
    for r in range(1, layer + 1):
        csum = csum + p[r:r + 1, :]
    lb = csum - p[0:1, :]

    def chunk(rows):
        fgate = lb + (1.0 - lb) * jax.nn.sigmoid(f_ref[rows, :])
        kin = 1.0 - fgate
        logf = jnp.log2(fgate)
        logf_hi = logf.astype(BF16)
        logf_lo = (logf - logf_hi.astype(F32)).astype(BF16)
        logf_split = jnp.concatenate([logf_hi, logf_lo], axis=0)
        decs = [jnp.dot(gm_ref[r * ch:(r + 1) * ch, :], logf_split, preferred_element_type=F32)
                for r in range(HG_LEVELS + 2)]

        row = lax.broadcasted_iota(I32, (ch, B_KEY_DIM), 0)
        split = split_ref[...]
        nt = (((1,), (1,)), ((), ()))
        heads = range(HG_HEADS_PER_STEP)
        levels = range(1, HG_LEVELS + 1)
        sls = [slice(hh * B_KEY_DIM, (hh + 1) * B_KEY_DIM) for hh in heads]

        qs = [q_ref[rows, sl] * (B_KEY_DIM ** -0.5) for sl in sls]
        ks = [kin[:, sl] for sl in sls]
        vs = [i_ref[rows, sl] for sl in sls]
        b_incl = [decs[HG_LEVELS][:, sl] for sl in sls]
        b_rest = [decs[HG_LEVELS + 1][:, sl] for sl in sls]
        states = [st_ref[hh] for hh in heads]

        zs = []
        for hh in heads:
            for l in levels:
                upper = (row & (1 << (l - 1))) != 0
                ex = jnp.exp2(decs[l - 1][:, sls[hh]])
                zs.append((jnp.where(upper, qs[hh], ks[hh]) * ex).astype(BF16))
        prods = [lax.dot_general(z, z, nt, preferred_element_type=F32) for z in zs]
        o_inter = [lax.dot_general((qs[hh] * jnp.exp2(b_incl[hh])).astype(BF16), states[hh].astype(BF16), nt,
                                   preferred_element_type=F32) for hh in heads]
        new_kv = [jnp.dot(vs[hh].T.astype(BF16), (ks[hh] * jnp.exp2(b_rest[hh])).astype(BF16),
                          preferred_element_type=F32) for hh in heads]
        a_mats = []
        for hh in heads:
            a = jnp.zeros((ch, ch), F32)
            for l in levels:
                a = jnp.where(split == l, prods[hh * HG_LEVELS + l - 1], a)
            a_mats.append(a.astype(BF16))
        o_intra = [jnp.dot(a_mats[hh], vs[hh].astype(BF16), preferred_element_type=F32) for hh in heads]

        for hh in heads:
            sl = sls[hh]
            st_ref[hh] = states[hh] * jnp.exp2(b_incl[hh][ch - 1:ch, :]) + new_kv[hh]
            o = o_inter[hh] + o_intra[hh] + jnp.sum(qs[hh] * ks[hh], axis=-1, keepdims=True) * vs[hh]
            on = o * _rms_scale(o) * og_ref[:, sl]
            gate = g_ref[rows, sl]
            o_ref[rows, sl] = (on * (gate * jax.nn.sigmoid(gate))).astype(o_ref.dtype)

    for cc in range(HG_CHUNKS_PER_STEP):
        chunk(slice(cc * ch, (cc + 1) * ch))


def _hgrn_core(proj, lb_logits, o_gain, layer):
    s = proj.shape[0]
    w = HG_HEADS_PER_STEP * B_KEY_DIM
    nhb = D_MODEL // w
    rows_per_step = HG_CHUNKS_PER_STEP * HG_CHUNK
    gm = jnp.asarray(np.tile(_hgrn_sum_matrices(), (1, 2)), dtype=BF16)
    split = jnp.asarray(_hgrn_split_levels())
    col = lambda off: (lambda hb, c: (c, off * nhb + hb))
    return pl.pallas_call(
        functools.partial(_hgrn_body, layer=layer),
        grid=(nhb, s // rows_per_step),
        in_specs=[
            pl.BlockSpec((rows_per_step, w), col(0)),
            pl.BlockSpec((rows_per_step, w), col(1)),
            pl.BlockSpec((rows_per_step, w), col(2)),
            pl.BlockSpec((rows_per_step, w), col(3)),
            pl.BlockSpec((DEPTH, w), lambda hb, c: (0, hb)),
            pl.BlockSpec((1, w), lambda hb, c: (0, hb)),
            pl.BlockSpec(gm.shape, lambda hb, c: (0, 0)),
            pl.BlockSpec(split.shape, lambda hb, c: (0, 0)),
        ],
        out_specs=pl.BlockSpec((rows_per_step, w), lambda hb, c: (c, hb)),
        out_shape=jax.ShapeDtypeStruct((s, D_MODEL), BF16),
        scratch_shapes=[pltpu.VMEM((HG_HEADS_PER_STEP, B_KEY_DIM, B_KEY_DIM), F32)],
        compiler_params=_params("parallel", "arbitrary"),
        name="hgrn_core",
    )(proj, proj, proj, proj, lb_logits, o_gain.reshape(1, D_MODEL), gm, split)


def _hgrn_mixer(x, gain, w_in, w_out, mixer_idx, lb_logits, o_gain, layer):
    proj = _hgrn_in_proj(x, gain, w_in, mixer_idx)
    o = _hgrn_core(proj, lb_logits, o_gain, layer)
    return _out_proj(o, w_out, mixer_idx, x)


def kernel(x, norm_gains, ffn_w_gate, ffn_w_up, ffn_w_down, dsa_w_in, dsa_w_out, dsa_q_gain, dsa_k_gain,
           dsa_idx_k_gain, rel_bias, hgrn_w_in, hgrn_w_out, hgrn_lb_logits, hgrn_o_gain):
    assert x.shape[0] == 1 and x.shape[2] == D_MODEL and x.shape[1] % FFN_TM == 0
    h = x.reshape(x.shape[1], D_MODEL)
    dsa_w_idx = jnp.pad(dsa_w_in[:, :, A_MAIN_COLS:], ((0, 0), (0, 0), (0, LANES - (A_IN_COLS - A_MAIN_COLS))))
    ik_gain = jnp.pad(dsa_idx_k_gain, ((0, 0), (0, LANES - IDX_DIM)))
    bias = _near_bias(rel_bias)

    for layer in range(DEPTH):
        j = layer // N_MIXERS
        h = _ffn(h, norm_gains[layer, 0], ffn_w_gate, ffn_w_up, ffn_w_down, (layer, 0))
        if layer % N_MIXERS == 0:
            h = _dsa_mixer(h, norm_gains[layer, 1], dsa_w_in, dsa_w_idx, dsa_w_out, j, dsa_q_gain[j], dsa_k_gain[j],
                           ik_gain[j:j + 1], bias)
        else:
            h = _hgrn_mixer(h, norm_gains[layer, 1], hgrn_w_in, hgrn_w_out, j, hgrn_lb_logits, hgrn_o_gain[j], layer)
        h = _ffn(h, norm_gains[layer, 2], ffn_w_gate, ffn_w_up, ffn_w_down, (layer, 1))
    return h.reshape(x.shape)
```

```python
import functools
import math

import numpy as np
import jax
import jax.numpy as jnp
from jax import lax
from jax.experimental import pallas as pl
from jax.experimental.pallas import tpu as pltpu

F32 = jnp.float32
BF16 = jnp.bfloat16
I32 = jnp.int32

D_MODEL = 2048
DEPTH = 4
N_MIXERS = 2
A_HEADS = 16
A_HEAD_DIM = 128
A_KV_HEADS = 4
A_GROUP = A_HEADS // A_KV_HEADS
IDX_HEADS = 16
IDX_DIM = 64
TOPK_MAX = 256
REL_BUCKETS = 32
REL_MAX_DIST = 128
B_KEY_DIM = 128
B_HEADS = D_MODEL // B_KEY_DIM
D_FF = 5632
EPS = 1e-6

A_Q_COLS = A_HEADS * A_HEAD_DIM
A_KV_COLS = A_KV_HEADS * A_HEAD_DIM
A_IQ_COLS = IDX_HEADS * IDX_DIM
A_MAIN_COLS = A_Q_COLS + 2 * A_KV_COLS + A_IQ_COLS
A_IN_COLS = A_MAIN_COLS + IDX_DIM + IDX_HEADS

LANES = 128
V7X_VMEM_BYTES = 64 * 1024 * 1024
VMEM_LIMIT_BYTES = 56 * 1024 * 1024
FFN_REST_VMEM_LIMIT_BYTES = 58 * 1024 * 1024

INT_MIN = -(2 ** 31)
LOG2_E = math.log2(math.e)
MASKED_LOGIT = -1e30
M_INIT = -1e20
VT_ROWS = A_HEAD_DIM + 16

FFN_TM = 1024
FFN_TF = 512
FFN_TF_FIRST = 256
FFN_COPY_ROWS = 128
NMM_TM = 1024
NMM_TN = 512
HG_IN_TN = 1024
NMM_ROWS = 256
OUT_TM = 512
QB = 128
KB = 2 * QB
SCORE_CHUNK = 512
KEY_BITS = 32
SUBLANES = 8
PLANE_KEYS = KEY_BITS * SUBLANES
COUNT_ROWS = 512
HG_CHUNK = 128
HG_HEADS_PER_STEP = 4
HG_CHUNKS_PER_STEP = 4
HG_LEVELS = 7


def _params(*semantics, vmem_limit_bytes=VMEM_LIMIT_BYTES):
    return pltpu.CompilerParams(dimension_semantics=semantics, vmem_limit_bytes=vmem_limit_bytes)


def _resident(shape, index_map):
    return pl.BlockSpec(shape, index_map, pipeline_mode=pl.Buffered(1))


def _rms_scale(x):
    return lax.rsqrt(jnp.mean(x * x, axis=-1, keepdims=True) + EPS)


def _ffn_step(j, n_steps, x_ref, g_ref, weights, o_ref, h_ref):
    @pl.when(j == 0)
    def _():
        x = x_ref[...]
        h_ref[...] = (x * _rms_scale(x) * g_ref[...]).astype(BF16)
        o_ref[...] = jnp.zeros_like(o_ref)

    h = h_ref[...]
    for wg, wu, wd in weights():
        gate = jnp.dot(h, wg, preferred_element_type=F32)
        up = jnp.dot(h, wu, preferred_element_type=F32)
        act = (gate * jax.nn.sigmoid(gate) * up).astype(BF16)
        o_ref[...] += jnp.dot(act, wd, preferred_element_type=F32)

    @pl.when(j == n_steps - 1)
    def _():
        o_ref[...] = x_ref[...] + 0.5 * o_ref[...]


def _ffn_first_body(x_ref, g_ref, wg_ref, wu_ref, wd_ref, y_ref, wgb_ref, wub_ref, wdb_ref, h_ref):
    def weights():
        wg = wg_ref[...].astype(BF16)
        wu = wu_ref[...].astype(BF16)
        wd = wd_ref[...].astype(BF16)
        wgb_ref[0] = wg
        wub_ref[0] = wu
        wdb_ref[...] = wd
        return [(wg, wu, wd)]

    _ffn_step(pl.program_id(0), pl.num_programs(0), x_ref, g_ref, weights, y_ref, h_ref)


def _ffn_rest_body(x_ref, y0_ref, g_ref, wg_ref, wu_ref, wd_ref, o_ref, h_ref):
    i = pl.program_id(0)
    j = pl.program_id(1)

    @pl.when(jnp.logical_and(i == 0, j < o_ref.shape[0] // FFN_COPY_ROWS))
    def _():
        o_ref[pl.ds(pl.multiple_of(j * FFN_COPY_ROWS, FFN_COPY_ROWS), FFN_COPY_ROWS), :] = y0_ref[...]

    def weights():
        return [(wg_ref[p], wu_ref[p], wd_ref[p * FFN_TF_FIRST:(p + 1) * FFN_TF_FIRST, :])
                for p in range(FFN_TF // FFN_TF_FIRST)]

    @pl.when(i > 0)
    def _():
        _ffn_step(j, pl.num_programs(1), x_ref, g_ref, weights, o_ref, h_ref)


def _ffn(x, gain, wg32, wu32, wd32, lead):
    s, d = x.shape
    f = wg32.shape[-1]
    tm = min(FFN_TM, s)
    squeeze = (None,) * len(lead)
    n_copy = tm // FFN_COPY_ROWS
    assert f // FFN_TF >= n_copy
    y0, wg, wu, wd = pl.pallas_call(
        _ffn_first_body,
        grid=(f // FFN_TF_FIRST,),
        in_specs=[
            _resident((tm, d), lambda j: (0, 0)),
            pl.BlockSpec((1, d), lambda j: (0, 0)),
            pl.BlockSpec(squeeze + (d, FFN_TF_FIRST), lambda j: lead + (0, j)),
            pl.BlockSpec(squeeze + (d, FFN_TF_FIRST), lambda j: lead + (0, j)),
            pl.BlockSpec(squeeze + (FFN_TF_FIRST, d), lambda j: lead + (j, 0)),
        ],
        out_specs=(
            pl.BlockSpec((tm, d), lambda j: (0, 0)),
            pl.BlockSpec((1, d, FFN_TF_FIRST), lambda j: (j, 0, 0)),
            pl.BlockSpec((1, d, FFN_TF_FIRST), lambda j: (j, 0, 0)),
            pl.BlockSpec((FFN_TF_FIRST, d), lambda j: (j, 0)),
        ),
        out_shape=(jax.ShapeDtypeStruct((tm, d), F32), jax.ShapeDtypeStruct((f // FFN_TF_FIRST, d, FFN_TF_FIRST), BF16),
                   jax.ShapeDtypeStruct((f // FFN_TF_FIRST, d, FFN_TF_FIRST), BF16), jax.ShapeDtypeStruct((f, d), BF16)),
        scratch_shapes=[pltpu.VMEM((tm, d), BF16)],
        compiler_params=_params("arbitrary"),
        name="ffn_first",
    )(x, gain.reshape(1, d), wg32, wu32, wd32)
    slabs = FFN_TF // FFN_TF_FIRST
    wcol = lambda i, j: (jnp.where(i > 0, j, 0), 0, 0)
    wrow = lambda i, j: (jnp.where(i > 0, j, 0), 0)
    return pl.pallas_call(
        _ffn_rest_body,
        grid=(s // tm, f // FFN_TF),
        in_specs=[
            pl.BlockSpec((tm, d), lambda i, j: (jnp.minimum(jnp.maximum(i, 1), s // tm - 1), 0)),
            pl.BlockSpec((FFN_COPY_ROWS, d), lambda i, j: (jnp.where(i == 0, jnp.minimum(j, n_copy - 1), n_copy - 1), 0)),
            pl.BlockSpec((1, d), lambda i, j: (0, 0)),
            pl.BlockSpec((slabs, d, FFN_TF_FIRST), wcol),
            pl.BlockSpec((slabs, d, FFN_TF_FIRST), wcol),
            pl.BlockSpec((FFN_TF, d), wrow),
        ],
        out_specs=pl.BlockSpec((tm, d), lambda i, j: (i, 0)),
        out_shape=jax.ShapeDtypeStruct((s, d), F32),
        scratch_shapes=[pltpu.VMEM((tm, d), BF16)],
        compiler_params=_params("arbitrary", "arbitrary", vmem_limit_bytes=FFN_REST_VMEM_LIMIT_BYTES),
        name="ffn_rest",
    )(x, y0, gain.reshape(1, d), wg, wu, wd)


def _nmm_body(*refs, n_extra, epilogue):
    x_ref, g_ref, w_ref = refs[:3]
    extra = refs[3:3 + n_extra]
    out_refs = refs[3 + n_extra:-1]
    h_ref = refs[-1]
    n = pl.program_id(1)

    @pl.when(n == 0)
    def _():
        x = x_ref[...]
        h_ref[...] = (x * _rms_scale(x) * g_ref[...]).astype(BF16)

    epilogue(n, h_ref, w_ref, extra, out_refs)


assert NMM_ROWS == KB


def _row_chunks(h_ref, w_ref, emit):
    w = w_ref[...].astype(BF16)
    for r in range(h_ref.shape[0] // NMM_ROWS):
        rows = slice(r * NMM_ROWS, (r + 1) * NMM_ROWS)
        emit(rows, jnp.dot(h_ref[rows, :], w, preferred_element_type=F32))


def _norm_matmul(x, gain, w, layer, n_tiles, extra, extra_specs, out_shapes, out_specs, epilogue, tn=NMM_TN):
    s, d = x.shape
    tm = min(NMM_TM, s)
    last_whole = w.shape[-1] // tn - 1
    return pl.pallas_call(
        functools.partial(_nmm_body, n_extra=len(extra), epilogue=epilogue),
        grid=(s // tm, n_tiles),
        in_specs=[
            pl.BlockSpec((tm, d), lambda i, n: (i, 0)),
            pl.BlockSpec((1, d), lambda i, n: (0, 0)),
            pl.BlockSpec((None, d, tn), lambda i, n: (layer, 0, jnp.minimum(n, last_whole))),
        ] + list(extra_specs),
        out_specs=out_specs,
        out_shape=out_shapes,
        scratch_shapes=[pltpu.VMEM((tm, d), BF16)],
        compiler_params=_params("parallel", "arbitrary"),
        name="norm_matmul",
    )(x, gain.reshape(1, d), w, *extra)


def _head_norm(y, gain, scale):
    parts = []
    for c in range(NMM_TN // A_HEAD_DIM):
        z = y[:, c * A_HEAD_DIM:(c + 1) * A_HEAD_DIM]
        parts.append(z * _rms_scale(z) * (gain * scale))
    return jnp.concatenate(parts, axis=1)


def _dsa_in_epilogue(n, h_ref, w_ref, extra, out_refs):
    qg_ref, kg_ref, ikg_ref, widx_ref = extra
    main_ref, iw_ref, vt_ref = out_refs
    q_tiles = A_Q_COLS // NMM_TN
    k_tile = q_tiles
    v_tile = k_tile + 1
    iq_tile0 = v_tile + 1
    idx_tile = A_MAIN_COLS // NMM_TN

    def store_main(fn):
        def emit(rows, y):
            main_ref[rows, :] = fn(y).astype(BF16)
        _row_chunks(h_ref, w_ref, emit)

    @pl.when(n < q_tiles)
    def _():
        store_main(lambda y: _head_norm(y, qg_ref[...], A_HEAD_DIM ** -0.5 * LOG2_E))

    @pl.when(n == k_tile)
    def _():
        store_main(lambda y: _head_norm(y, kg_ref[...], 1.0))

    @pl.when(n == v_tile)
    def _():
        tail = (lax.broadcasted_iota(I32, (VT_ROWS - A_HEAD_DIM, KB), 0) == 0).astype(BF16)

        def emit(rows, y):
            main_ref[rows, :] = y.astype(BF16)
            blk = rows.start // KB
            for j in range(A_KV_HEADS):
                vt_ref[blk, j * VT_ROWS:j * VT_ROWS + A_HEAD_DIM, :] = (
                    y[:, j * A_HEAD_DIM:(j + 1) * A_HEAD_DIM].T.astype(BF16))
                vt_ref[blk, j * VT_ROWS + A_HEAD_DIM:(j + 1) * VT_ROWS, :] = tail
        _row_chunks(h_ref, w_ref, emit)

    @pl.when(jnp.logical_and(n >= iq_tile0, n < idx_tile))
    def _():
        store_main(lambda y: y * IDX_DIM ** -0.5)

    @pl.when(n == idx_tile)
    def _():
        def emit(rows, z):
            lane = lax.broadcasted_iota(I32, z.shape, 1)
            ms = jnp.sum(jnp.where(lane < IDX_DIM, z * z, 0.0), axis=-1, keepdims=True) * (1.0 / IDX_DIM)
            ikn = z * lax.rsqrt(ms + EPS) * ikg_ref[...]
            main_ref[rows, :] = jnp.concatenate(
                [ikn, jnp.zeros((z.shape[0], NMM_TN - LANES), F32)], axis=1).astype(BF16)
            iw_ref[rows, :] = z * IDX_HEADS ** -0.5
        _row_chunks(h_ref, widx_ref, emit)


def _dsa_in_proj(x, gain, w_in, w_idx, layer, q_gain, k_gain, ik_gain_padded):
    s, d = x.shape
    tm = min(NMM_TM, s)
    n_tiles = A_MAIN_COLS // NMM_TN + 1
    ncols = n_tiles * NMM_TN
    vec = lambda i, n: (0, 0)
    return _norm_matmul(
        x, gain, w_in, layer, n_tiles,
        extra=(q_gain.reshape(1, A_HEAD_DIM), k_gain.reshape(1, A_HEAD_DIM), ik_gain_padded, w_idx),
        extra_specs=[pl.BlockSpec((1, A_HEAD_DIM), vec), pl.BlockSpec((1, A_HEAD_DIM), vec),
                     pl.BlockSpec((1, LANES), vec), pl.BlockSpec((None, d, LANES), lambda i, n: (layer, 0, 0))],
        out_shapes=(jax.ShapeDtypeStruct((s, ncols), BF16), jax.ShapeDtypeStruct((s, LANES), F32),
                    jax.ShapeDtypeStruct((s // KB, A_KV_HEADS * VT_ROWS, KB), BF16)),
        out_specs=(pl.BlockSpec((tm, NMM_TN), lambda i, n: (i, n)),
                   pl.BlockSpec((tm, LANES), lambda i, n: (i, 0)),
                   pl.BlockSpec((tm // KB, A_KV_HEADS * VT_ROWS, KB), lambda i, n: (i, 0, 0))),
        epilogue=_dsa_in_epilogue,
    )


def _identity_epilogue(n, h_ref, w_ref, extra, out_refs):
    def emit(rows, y):
        out_refs[0][rows, :] = y
    _row_chunks(h_ref, w_ref, emit)


def _hgrn_in_proj(x, gain, w, layer):
    s = x.shape[0]
    tm = min(NMM_TM, s)
    return _norm_matmul(
        x, gain, w, layer, w.shape[-1] // HG_IN_TN, extra=(), extra_specs=[],
        out_shapes=jax.ShapeDtypeStruct((s, w.shape[-1]), F32),
        out_specs=pl.BlockSpec((tm, HG_IN_TN), lambda i, n: (i, n)),
        epilogue=_identity_epilogue, tn=HG_IN_TN,
    )


def _out_proj_body(a_ref, w_ref, r_ref, o_ref, wb_ref):
    @pl.when(pl.program_id(0) == 0)
    def _():
        wb_ref[...] = w_ref[...].astype(BF16)

    o_ref[...] = r_ref[...] + jnp.dot(a_ref[...], wb_ref[...], preferred_element_type=F32)


def _out_proj(a, w, layer, res):
    s, k = a.shape
    d = w.shape[-1]
    tm = min(OUT_TM, s)
    return pl.pallas_call(
        _out_proj_body,
        grid=(s // tm,),
        in_specs=[
            pl.BlockSpec((tm, k), lambda i: (i, 0)),
            _resident((None, k, d), lambda i: (layer, 0, 0)),
            pl.BlockSpec((tm, d), lambda i: (i, 0)),
        ],
        out_specs=pl.BlockSpec((tm, d), lambda i: (i, 0)),
        out_shape=jax.ShapeDtypeStruct((s, d), F32),
        scratch_shapes=[pltpu.VMEM((k, d), BF16)],
        compiler_params=_params("arbitrary"),
        name="out_proj",
    )(a, w, res)


def _t5_bucket_table():
    exact = REL_BUCKETS // 2
    d = np.arange(2 * QB)
    df = np.maximum(d, 1).astype(np.float64)
    val = np.log(df / exact) / math.log(REL_MAX_DIST / exact) * (REL_BUCKETS - exact)
    big = d >= exact
    frac = np.abs(val[big] - np.round(val[big]))
    assert np.all((frac > 1e-4) | (frac == 0.0))
    large = np.minimum(exact + np.floor(val + 1e-9).astype(np.int64), REL_BUCKETS - 1)
    return np.where(d < exact, d, large).astype(np.int32)


def _near_bucket_tiles():
    table = _t5_bucket_table()
    r = np.arange(QB)[:, None]
    t = np.arange(QB)[None, :]
    tiles = [table[np.maximum(n * QB + t - r, 0)] for n in range(2)]
    return np.stack(tiles).astype(np.int32)


def _bias_body(rel_ref, bkt_ref, o_ref):
    j = pl.program_id(1)
    bkt = bkt_ref[0]
    for g in range(A_GROUP):
        h = j * A_GROUP + g
        far = rel_ref[REL_BUCKETS - 1, h]
        acc = jnp.zeros(bkt.shape, F32)
        for b in range(REL_BUCKETS):
            acc = jnp.where(bkt == b, (rel_ref[b, h] - far) * LOG2_E, acc)
        o_ref[0, 0, :, g * QB:(g + 1) * QB] = acc


def _near_bias(rel_bias):
    bkt = jnp.asarray(_near_bucket_tiles())
    return pl.pallas_call(
        _bias_body,
        grid=(2, A_KV_HEADS),
        in_specs=[
            pl.BlockSpec(memory_space=pltpu.SMEM),
            pl.BlockSpec((1, QB, QB), lambda n, j: (n, 0, 0)),
        ],
        out_specs=pl.BlockSpec((1, 1, QB, A_GROUP * QB), lambda n, j: (n, j, 0, 0)),
        out_shape=jax.ShapeDtypeStruct((2, A_KV_HEADS, QB, A_GROUP * QB), F32),
        compiler_params=_params("arbitrary", "arbitrary"),
        name="near_bias",
    )(rel_bias, bkt)


def _bit_transpose32(words):
    a = list(words)
    for j, mask in ((16, 0x0000FFFF), (8, 0x00FF00FF), (4, 0x0F0F0F0F), (2, 0x33333333), (1, 0x55555555)):
        shift = jnp.full(a[0].shape, j, I32)
        for k in range(32):
            if k & j == 0:
                t = (lax.shift_right_logical(a[k], shift) ^ a[k + j]) & mask
                a[k + j] = a[k + j] ^ t
                a[k] = a[k] ^ lax.shift_left(t, shift)
    return a


def _attn_body(q_ref, iq_ref, iw_ref, k_ref, ik_ref, vt_ref, bias_ref, o_ref,
               iqt_ref, qt_ref, sc_ref, plane_ref, eq_ref, hit_ref, st_ref, bmax_ref, acc_ref, m_ref, *, topk):
    i = pl.program_id(0)
    gq = A_GROUP * QB
    POS_BITS = (sc_ref.shape[0] - 1).bit_length()

    iqt = iq_ref[...].astype(F32).T
    iqt_ref[...] = jnp.zeros_like(iqt_ref)
    for h in range(IDX_HEADS):
        iqt_ref[0:IDX_DIM, h * QB:(h + 1) * QB] = iqt[h * IDX_DIM:(h + 1) * IDX_DIM, :].astype(BF16)
    qt = q_ref[...].astype(F32).T
    eye = (lax.broadcasted_iota(I32, (QB, QB), 0) == lax.broadcasted_iota(I32, (QB, QB), 1)).astype(BF16)
    for h in range(A_HEADS):
        qt_ref[0:A_HEAD_DIM, h * QB:(h + 1) * QB] = qt[h * A_HEAD_DIM:(h + 1) * A_HEAD_DIM, :].astype(BF16)
        qt_ref[A_HEAD_DIM:, h * QB:(h + 1) * QB] = eye
    iwt = iw_ref[...].T

    n_chunks = (i * QB + QB + SCORE_CHUNK - 1) // SCORE_CHUNK

    def score_chunk(c, causal_mask):
        row0 = pl.multiple_of(c * SCORE_CHUNK, SCORE_CHUNK)
        ikc = ik_ref[pl.ds(row0, SCORE_CHUNK), :]
        sc = jnp.zeros((SCORE_CHUNK, QB), F32)
        for hp in range(IDX_HEADS // 2):
            raw = jnp.dot(ikc, iqt_ref[:, hp * 2 * QB:(hp + 1) * 2 * QB], preferred_element_type=F32)
            for u in range(2):
                h = 2 * hp + u
                sc = sc + jnp.maximum(raw[:, u * QB:(u + 1) * QB], 0.0) * iwt[IDX_DIM + h:IDX_DIM + h + 1, :]
        bits = pltpu.bitcast(sc, I32)
        skey = jnp.where(bits >= 0, bits, bits ^ 0x7FFFFFFF)
        if causal_mask:
            key_pos = row0 + lax.broadcasted_iota(I32, sc.shape, 0)
            q_pos = i * QB + lax.broadcasted_iota(I32, sc.shape, 1)
            skey = jnp.where(key_pos <= q_pos, skey, INT_MIN)
        sc_ref[pl.ds(row0, SCORE_CHUNK), :] = skey
        ukey = skey ^ INT_MIN
        for gi in range(SCORE_CHUNK // PLANE_KEYS):
            base = gi * PLANE_KEYS
            planes = _bit_transpose32([ukey[base + SUBLANES * j:base + SUBLANES * (j + 1), :]
                                       for j in range(KEY_BITS)])
            for b in range(KEY_BITS):
                plane_ref[c * (SCORE_CHUNK // PLANE_KEYS) + gi, b] = planes[b]

    def full_chunk(c, carry):
        score_chunk(c, False)
        return carry

    lax.fori_loop(0, n_chunks - 1, full_chunk, 0)
    score_chunk(n_chunks - 1, True)

    eq_ref[...] = jnp.full_like(eq_ref, -1)
    hit_ref[...] = jnp.full_like(hit_ref, -1)
    groups_per_iter = SCORE_CHUNK // PLANE_KEYS

    def settle(g, keep_prev):
        hit = hit_ref[g]
        return jnp.where(keep_prev != 0, hit, eq_ref[g] ^ hit)

    def bit_pass(it, carry):
        chosen, n_greater, keep_prev = carry
        bit = KEY_BITS - 1 - it

        def sweep(gg, acc):
            for u in range(groups_per_iter):
                g = gg * groups_per_iter + u
                eq = settle(g, keep_prev)
                hit = eq & plane_ref[g, bit]
                eq_ref[g] = eq
                hit_ref[g] = hit
                acc = acc + lax.population_count(hit)
            return acc

        acc = lax.fori_loop(0, n_chunks, sweep, jnp.zeros((SUBLANES, QB), I32))
        n_ge = n_greater + jnp.sum(acc, axis=0, keepdims=True)
        keep = n_ge >= topk
        chosen = jnp.where(keep, chosen | (jnp.int32(1) << bit), chosen)
        n_greater = jnp.where(keep, n_greater, n_ge)
        return chosen, n_greater, keep.astype(I32)

    zeros_row = jnp.zeros((1, QB), I32)
    chosen, n_greater, keep_last = lax.fori_loop(0, KEY_BITS, bit_pass, (zeros_row, zeros_row, zeros_row + 1))
    thr = jnp.maximum(chosen ^ INT_MIN, INT_MIN + 1)

    def count_equal(gg, acc):
        for u in range(groups_per_iter):
            acc = acc + lax.population_count(settle(gg * groups_per_iter + u, keep_last))
        return acc

    n_equal = jnp.sum(lax.fori_loop(0, n_chunks, count_equal, jnp.zeros((SUBLANES, QB), I32)), axis=0, keepdims=True)
    n_ge_thr = jnp.where(chosen == 0, 0, n_greater + n_equal)

    n_count = n_chunks * (SCORE_CHUNK // COUNT_ROWS)

    def count_keys(pred):
        def body(r, acc):
            row0 = pl.multiple_of(r * COUNT_ROWS, COUNT_ROWS)
            hit = jnp.where(pred(sc_ref[pl.ds(row0, COUNT_ROWS), :], row0), 1, 0).astype(I32)
            return acc + jnp.sum(hit.reshape(COUNT_ROWS // SUBLANES, SUBLANES, QB), axis=0)
        acc = lax.fori_loop(0, n_count, body, jnp.zeros((SUBLANES, QB), I32))
        return jnp.sum(acc, axis=0, keepdims=True)

    @pl.when(jnp.max(n_ge_thr) > topk)
    def _():
        need = topk - n_greater

        def key_pos_of(blk, row0):
            return row0 + lax.broadcasted_iota(I32, blk.shape, 0)

        def pos_step(it, last):
            cand = last | (jnp.int32(1) << (POS_BITS - 1 - it))
            n_before = count_keys(lambda blk, row0: jnp.logical_and(blk == thr, key_pos_of(blk, row0) < cand))
            return jnp.where(n_before < need, cand, last)

        last = lax.fori_loop(0, POS_BITS, pos_step, jnp.zeros((1, QB), I32))

        def demote(r, carry):
            row0 = pl.multiple_of(r * COUNT_ROWS, COUNT_ROWS)
            blk = sc_ref[pl.ds(row0, COUNT_ROWS), :]
            late_tie = jnp.logical_and(blk == thr, key_pos_of(blk, row0) > last)
            sc_ref[pl.ds(row0, COUNT_ROWS), :] = jnp.where(late_tie, thr - 1, blk)
            return carry

        lax.fori_loop(0, n_count, demote, 0)

    m_ref[...] = jnp.full_like(m_ref, M_INIT)
    acc_ref[...] = jnp.zeros_like(acc_ref)
    heads = range(A_KV_HEADS)

    def logits(c, slot):
        row0 = pl.multiple_of(c * KB, KB)
        sel = sc_ref[pl.ds(row0, KB), :] >= thr
        mask_cols = jnp.where(sel, 0.0, MASKED_LOGIT).astype(BF16)
        kblk = k_ref[pl.ds(row0, KB), :]
        for j in heads:
            kaug = jnp.concatenate([kblk[:, j * A_HEAD_DIM:(j + 1) * A_HEAD_DIM], mask_cols], axis=1)
            st = jnp.dot(kaug, qt_ref[:, j * gq:(j + 1) * gq], preferred_element_type=F32)
            st_ref[slot, j] = st
            bmax_ref[slot, j] = jnp.max(st, axis=0, keepdims=True)

    def softmax_pv(c, slot, near):
        ps, alphas = [], []
        for j in heads:
            st = st_ref[slot, j]
            if near != (None, None):
                halves = [st[0:QB], st[QB:KB]]
                halves = [hv if n is None else hv + bias_ref[n, j] for hv, n in zip(halves, near)]
                st = jnp.concatenate(halves, axis=0)
                block_max = jnp.max(st, axis=0, keepdims=True)
            else:
                block_max = bmax_ref[slot, j]
            m_old = m_ref[j]
            m_new = jnp.maximum(m_old, block_max)
            m_ref[j] = m_new
            ps.append(jnp.exp2(st - m_new).astype(BF16))
            alphas.append(jnp.exp2(m_old - m_new))
        vblk = vt_ref[c]
        pvs = [jnp.dot(vblk[j * VT_ROWS:(j + 1) * VT_ROWS, :], ps[j], preferred_element_type=F32)
               for j in heads]
        for j in heads:
            acc_ref[j] = acc_ref[j] * alphas[j] + pvs[j]

    far = (None, None)
    n_far = jnp.maximum(lax.shift_right_arithmetic(i - 1, 1), 0)
    i_odd = (i & 1) == 1

    @pl.when(i >= 1)
    def _():
        logits(0, 0)

    @pl.when(i == 0)
    def _():
        logits(0, 1)

    def far_pair(pp, carry):
        c0 = 2 * pp
        logits(c0 + 1, 1)
        softmax_pv(c0, 0, far)
        logits(c0 + 2, 0)
        softmax_pv(c0 + 1, 1, far)
        return carry

    lax.fori_loop(0, lax.shift_right_arithmetic(n_far, 1), far_pair, 0)

    @pl.when((n_far & 1) == 1)
    def _():
        softmax_pv(n_far - 1, 0, far)
        logits(n_far, 0)

    @pl.when(i_odd)
    def _():
        softmax_pv(n_far, 0, (1, 0))

    @pl.when(jnp.logical_and(jnp.logical_not(i_odd), i >= 2))
    def _():
        logits(n_far + 1, 1)
        softmax_pv(n_far, 0, (None, 1))

    @pl.when(jnp.logical_not(i_odd))
    def _():
        softmax_pv(lax.shift_right_arithmetic(i, 1), 1, (0, None))

    for j in heads:
        acc = acc_ref[j]
        out_t = acc[0:A_HEAD_DIM, :] * (1.0 / acc[A_HEAD_DIM:A_HEAD_DIM + 1, :])
        for g in range(A_GROUP):
            h = j * A_GROUP + g
            o_ref[:, h * A_HEAD_DIM:(h + 1) * A_HEAD_DIM] = out_t[:, g * QB:(g + 1) * QB].T.astype(o_ref.dtype)


def _dsa_attention(main, iw, vt3, bias, topk):
    s = main.shape[0]
    nb = s // QB
    return pl.pallas_call(
        functools.partial(_attn_body, topk=topk),
        grid=(nb,),
        in_specs=[
            pl.BlockSpec((QB, A_Q_COLS), lambda i: (i, 0)),
            pl.BlockSpec((QB, A_IQ_COLS), lambda i: (i, (A_Q_COLS + 2 * A_KV_COLS) // A_IQ_COLS)),
            pl.BlockSpec((QB, LANES), lambda i: (i, 0)),
            _resident((s, A_KV_COLS), lambda i: (0, A_Q_COLS // A_KV_COLS)),
            _resident((s, LANES), lambda i: (0, A_MAIN_COLS // LANES)),
            _resident((s // KB, A_KV_HEADS * VT_ROWS, KB), lambda i: (0, 0, 0)),
            _resident((2, A_KV_HEADS, QB, A_GROUP * QB), lambda i: (0, 0, 0, 0)),
        ],
        out_specs=pl.BlockSpec((QB, A_Q_COLS), lambda i: (i, 0)),
        out_shape=jax.ShapeDtypeStruct((s, A_Q_COLS), BF16),
        scratch_shapes=[
            pltpu.VMEM((LANES, IDX_HEADS * QB), BF16),
            pltpu.VMEM((A_HEAD_DIM + QB, A_HEADS * QB), BF16),
            pltpu.VMEM((s, QB), I32),
            pltpu.VMEM((s // PLANE_KEYS, KEY_BITS, SUBLANES, QB), I32),
            pltpu.VMEM((s // PLANE_KEYS, SUBLANES, QB), I32),
            pltpu.VMEM((s // PLANE_KEYS, SUBLANES, QB), I32),
            pltpu.VMEM((2, A_KV_HEADS, KB, A_GROUP * QB), F32),
            pltpu.VMEM((2, A_KV_HEADS, 1, A_GROUP * QB), F32),
            pltpu.VMEM((A_KV_HEADS, VT_ROWS, A_GROUP * QB), F32),
            pltpu.VMEM((A_KV_HEADS, 1, A_GROUP * QB), F32),
        ],
        compiler_params=_params("arbitrary"),
        name="dsa_attention",
    )(main, main, iw, main, main, vt3, bias)


def _dsa_mixer(x, gain, w_in, w_idx, w_out, layer, q_gain, k_gain, ik_gain_padded, bias):
    s = x.shape[0]
    topk = min(TOPK_MAX, s // 4)
    main, iw, vt3 = _dsa_in_proj(x, gain, w_in, w_idx, layer, q_gain, k_gain, ik_gain_padded)
    o = _dsa_attention(main, iw, vt3, bias, topk)
    return _out_proj(o, w_out, layer, x)


def _hgrn_sum_matrices():
    c = HG_CHUNK
    t = np.arange(c)[:, None]
    s = np.arange(c)[None, :]
    mats = []
    for l in range(1, HG_LEVELS + 1):
        blk = 1 << l
        m = (t // blk) * blk + blk // 2
        upper = t >= m
        mats.append(np.where(upper, (s >= m) & (s <= t), (s > t) & (s <= m - 1)))
    mats.append(s <= t)
    mats.append(s > t)
    return np.concatenate(mats, axis=0).astype(np.float32)


def _hgrn_split_levels():
    t = np.arange(HG_CHUNK)[:, None]
    s = np.arange(HG_CHUNK)[None, :]
    lvl = np.zeros((HG_CHUNK, HG_CHUNK), np.int32)
    for l in range(1, HG_LEVELS + 1):
        lvl[(s < t) & ((t >> l) == (s >> l)) & (((t >> (l - 1)) & 1) == 1) & (((s >> (l - 1)) & 1) == 0)] = l
    assert np.all((lvl > 0) == (s < t))
    return lvl


def _hgrn_body(q_ref, f_ref, i_ref, g_ref, lbl_ref, og_ref, gm_ref, split_ref, o_ref, st_ref, *, layer):
    c = pl.program_id(1)
    ch = HG_CHUNK

    @pl.when(c == 0)
    def _():
        st_ref[...] = jnp.zeros_like(st_ref)

    logits = lbl_ref[...]
    e = jnp.exp(logits - jnp.max(logits, axis=0, keepdims=True))
    p = e / jnp.sum(e, axis=0, keepdims=True)
    csum = p[0:1, :]
    for r in range(1, layer + 1):
        csum = csum + p[r:r + 1, :]
    lb = csum - p[0:1, :]

    def chunk(rows):
        fgate = lb + (1.0 - lb) * jax.nn.sigmoid(f_ref[rows, :])
        kin = 1.0 - fgate
        logf = jnp.log2(fgate)
        logf_hi = logf.astype(BF16)
        logf_lo = (logf - logf_hi.astype(F32)).astype(BF16)
        logf_split = jnp.concatenate([logf_hi, logf_lo], axis=0)
        decs = [jnp.dot(gm_ref[r * ch:(r + 1) * ch, :], logf_split, preferred_element_type=F32)
                for r in range(HG_LEVELS + 2)]

        row = lax.broadcasted_iota(I32, (ch, B_KEY_DIM), 0)
        split = split_ref[...]
        nt = (((1,), (1,)), ((), ()))
        heads = range(HG_HEADS_PER_STEP)
        levels = range(1, HG_LEVELS + 1)
        sls = [slice(hh * B_KEY_DIM, (hh + 1) * B_KEY_DIM) for hh in heads]

        qs = [q_ref[rows, sl] * (B_KEY_DIM ** -0.5) for sl in sls]
        ks = [kin[:, sl] for sl in sls]
        vs = [i_ref[rows, sl] for sl in sls]
        b_incl = [decs[HG_LEVELS][:, sl] for sl in sls]
        b_rest = [decs[HG_LEVELS + 1][:, sl] for sl in sls]
        states = [st_ref[hh] for hh in heads]

        zs = []
        for hh in heads:
            for l in levels:
                upper = (row & (1 << (l - 1))) != 0
                ex = jnp.exp2(decs[l - 1][:, sls[hh]])
                zs.append((jnp.where(upper, qs[hh], ks[hh]) * ex).astype(BF16))
        prods = [lax.dot_general(z, z, nt, preferred_element_type=F32) for z in zs]
        o_inter = [lax.dot_general((qs[hh] * jnp.exp2(b_incl[hh])).astype(BF16), states[hh].astype(BF16), nt,
                                   preferred_element_type=F32) for hh in heads]
        new_kv = [jnp.dot(vs[hh].T.astype(BF16), (ks[hh] * jnp.exp2(b_rest[hh])).astype(BF16),
                          preferred_element_type=F32) for hh in heads]
        a_mats = []
        for hh in heads:
            a = jnp.zeros((ch, ch), F32)
            for l in levels:
                a = jnp.where(split == l, prods[hh * HG_LEVELS + l - 1], a)
            a_mats.append(a.astype(BF16))
        o_intra = [jnp.dot(a_mats[hh], vs[hh].astype(BF16), preferred_element_type=F32) for hh in heads]

        for hh in heads:
            sl = sls[hh]
            st_ref[hh] = states[hh] * jnp.exp2(b_incl[hh][ch - 1:ch, :]) + new_kv[hh]
            o = o_inter[hh] + o_intra[hh] + jnp.sum(qs[hh] * ks[hh], axis=-1, keepdims=True) * vs[hh]
            on = o * _rms_scale(o) * og_ref[:, sl]
            gate = g_ref[rows, sl]
            o_ref[rows, sl] = (on * (gate * jax.nn.sigmoid(gate))).astype(o_ref.dtype)

    for cc in range(HG_CHUNKS_PER_STEP):
        chunk(slice(cc * ch, (cc + 1) * ch))


def _hgrn_core(proj, lb_logits, o_gain, layer):
    s = proj.shape[0]
    w = HG_HEADS_PER_STEP * B_KEY_DIM
    nhb = D_MODEL // w
    rows_per_step = HG_CHUNKS_PER_STEP * HG_CHUNK
    gm = jnp.asarray(np.tile(_hgrn_sum_matrices(), (1, 2)), dtype=BF16)
    split = jnp.asarray(_hgrn_split_levels())
    col = lambda off: (lambda hb, c: (c, off * nhb + hb))
    return pl.pallas_call(
        functools.partial(_hgrn_body, layer=layer),
        grid=(nhb, s // rows_per_step),
        in_specs=[
            pl.BlockSpec((rows_per_step, w), col(0)),
            pl.BlockSpec((rows_per_step, w), col(1)),
            pl.BlockSpec((rows_per_step, w), col(2)),
            pl.BlockSpec((rows_per_step, w), col(3)),
            pl.BlockSpec((DEPTH, w), lambda hb, c: (0, hb)),
            pl.BlockSpec((1, w), lambda hb, c: (0, hb)),
            pl.BlockSpec(gm.shape, lambda hb, c: (0, 0)),
            pl.BlockSpec(split.shape, lambda hb, c: (0, 0)),
        ],
        out_specs=pl.BlockSpec((rows_per_step, w), lambda hb, c: (c, hb)),
        out_shape=jax.ShapeDtypeStruct((s, D_MODEL), BF16),
        scratch_shapes=[pltpu.VMEM((HG_HEADS_PER_STEP, B_KEY_DIM, B_KEY_DIM), F32)],
        compiler_params=_params("parallel", "arbitrary"),
        name="hgrn_core",
    )(proj, proj, proj, proj, lb_logits, o_gain.reshape(1, D_MODEL), gm, split)


def _hgrn_mixer(x, gain, w_in, w_out, mixer_idx, lb_logits, o_gain, layer):
    proj = _hgrn_in_proj(x, gain, w_in, mixer_idx)
    o = _hgrn_core(proj, lb_logits, o_gain, layer)
    return _out_proj(o, w_out, mixer_idx, x)


def kernel(x, norm_gains, ffn_w_gate, ffn_w_up, ffn_w_down, dsa_w_in, dsa_w_out, dsa_q_gain, dsa_k_gain,
           dsa_idx_k_gain, rel_bias, hgrn_w_in, hgrn_w_out, hgrn_lb_logits, hgrn_o_gain):
    assert x.shape[0] == 1 and x.shape[2] == D_MODEL and x.shape[1] % FFN_TM == 0
    h = x.reshape(x.shape[1], D_MODEL)
    dsa_w_idx = jnp.pad(dsa_w_in[:, :, A_MAIN_COLS:], ((0, 0), (0, 0), (0, LANES - (A_IN_COLS - A_MAIN_COLS))))
    ik_gain = jnp.pad(dsa_idx_k_gain, ((0, 0), (0, LANES - IDX_DIM)))
    bias = _near_bias(rel_bias)

    for layer in range(DEPTH):
        j = layer // N_MIXERS
        h = _ffn(h, norm_gains[layer, 0], ffn_w_gate, ffn_w_up, ffn_w_down, (layer, 0))
        if layer % N_MIXERS == 0:
            h = _dsa_mixer(h, norm_gains[layer, 1], dsa_w_in, dsa_w_idx, dsa_w_out, j, dsa_q_gain[j], dsa_k_gain[j],
                           ik_gain[j:j + 1], bias)
        else:
            h = _hgrn_mixer(h, norm_gains[layer, 1], hgrn_w_in, hgrn_w_out, j, hgrn_lb_logits, hgrn_o_gain[j], layer)
        h = _ffn(h, norm_gains[layer, 2], ffn_w_gate, ffn_w_up, ffn_w_down, (layer, 1))
    return h.reshape(x.shape)
```

```python
import functools
import math

import numpy as np
import jax
import jax.numpy as jnp
from jax import lax
from jax.experimental import pallas as pl
from jax.experimental.pallas import tpu as pltpu

F32 = jnp.float32
BF16 = jnp.bfloat16
I32 = jnp.int32

D_MODEL = 2048
DEPTH = 4
N_MIXERS = 2
A_HEADS = 16
A_HEAD_DIM = 128
A_KV_HEADS = 4
A_GROUP = A_HEADS // A_KV_HEADS
IDX_HEADS = 16
IDX_DIM = 64
TOPK_MAX = 256
REL_BUCKETS = 32
REL_MAX_DIST = 128
B_KEY_DIM = 128
B_HEADS = D_MODEL // B_KEY_DIM
D_FF = 5632
EPS = 1e-6

A_Q_COLS = A_HEADS * A_HEAD_DIM
A_KV_COLS = A_KV_HEADS * A_HEAD_DIM
A_IQ_COLS = IDX_HEADS * IDX_DIM
A_MAIN_COLS = A_Q_COLS + 2 * A_KV_COLS + A_IQ_COLS
A_IN_COLS = A_MAIN_COLS + IDX_DIM + IDX_HEADS

LANES = 128
V7X_VMEM_BYTES = 64 * 1024 * 1024
VMEM_LIMIT_BYTES = 56 * 1024 * 1024
FFN_REST_VMEM_LIMIT_BYTES = 58 * 1024 * 1024

INT_MIN = -(2 ** 31)
LOG2_E = math.log2(math.e)
MASKED_LOGIT = -1e30
M_INIT = -1e20
VT_ROWS = A_HEAD_DIM + 16

FFN_TM = 1024
FFN_TF = 512
FFN_TF_FIRST = 256
FFN_COPY_ROWS = 128
NMM_TM = 1024
NMM_TN = 512
HG_IN_TN = 512
HG_IN_TM = 2048
NMM_ROWS = 256
OUT_TM = 512
QB = 128
KB = 2 * QB
SCORE_CHUNK = 512
KEY_BITS = 32
SUBLANES = 8
PLANE_KEYS = KEY_BITS * SUBLANES
COUNT_ROWS = 512
HG_CHUNK = 128
HG_HEADS_PER_STEP = 4
HG_CHUNKS_PER_STEP = 4
HG_LEVELS = 7


def _params(*semantics, vmem_limit_bytes=VMEM_LIMIT_BYTES):
    return pltpu.CompilerParams(dimension_semantics=semantics, vmem_limit_bytes=vmem_limit_bytes)


def _resident(shape, index_map):
    return pl.BlockSpec(shape, index_map, pipeline_mode=pl.Buffered(1))


def _rms_scale(x):
    return lax.rsqrt(jnp.mean(x * x, axis=-1, keepdims=True) + EPS)


def _ffn_step(j, n_steps, x_ref, g_ref, weights, o_ref, h_ref):
    @pl.when(j == 0)
    def _():
        x = x_ref[...]
        h_ref[...] = (x * _rms_scale(x) * g_ref[...]).astype(BF16)
        o_ref[...] = jnp.zeros_like(o_ref)

    h = h_ref[...]
    for wg, wu, wd in weights():
        gate = jnp.dot(h, wg, preferred_element_type=F32)
        up = jnp.dot(h, wu, preferred_element_type=F32)
        act = (gate * jax.nn.sigmoid(gate) * up).astype(BF16)
        o_ref[...] += jnp.dot(act, wd, preferred_element_type=F32)

    @pl.when(j == n_steps - 1)
    def _():
        o_ref[...] = x_ref[...] + 0.5 * o_ref[...]


def _ffn_first_body(x_ref, g_ref, wg_ref, wu_ref, wd_ref, y_ref, wgb_ref, wub_ref, wdb_ref, h_ref):
    def weights():
        wg = wg_ref[...].astype(BF16)
        wu = wu_ref[...].astype(BF16)
        wd = wd_ref[...].astype(BF16)
        wgb_ref[0] = wg
        wub_ref[0] = wu
        wdb_ref[...] = wd
        return [(wg, wu, wd)]

    _ffn_step(pl.program_id(0), pl.num_programs(0), x_ref, g_ref, weights, y_ref, h_ref)


def _ffn_rest_body(x_ref, y0_ref, g_ref, wg_ref, wu_ref, wd_ref, o_ref, h_ref):
    i = pl.program_id(0)
    j = pl.program_id(1)

    @pl.when(jnp.logical_and(i == 0, j < o_ref.shape[0] // FFN_COPY_ROWS))
    def _():
        o_ref[pl.ds(pl.multiple_of(j * FFN_COPY_ROWS, FFN_COPY_ROWS), FFN_COPY_ROWS), :] = y0_ref[...]

    def weights():
        return [(wg_ref[p], wu_ref[p], wd_ref[p * FFN_TF_FIRST:(p + 1) * FFN_TF_FIRST, :])
                for p in range(FFN_TF // FFN_TF_FIRST)]

    @pl.when(i > 0)
    def _():
        _ffn_step(j, pl.num_programs(1), x_ref, g_ref, weights, o_ref, h_ref)


def _ffn(x, gain, wg32, wu32, wd32, lead):
    s, d = x.shape
    f = wg32.shape[-1]
    tm = min(FFN_TM, s)
    squeeze = (None,) * len(lead)
    n_copy = tm // FFN_COPY_ROWS
    assert f // FFN_TF >= n_copy
    y0, wg, wu, wd = pl.pallas_call(
        _ffn_first_body,
        grid=(f // FFN_TF_FIRST,),
        in_specs=[
            _resident((tm, d), lambda j: (0, 0)),
            pl.BlockSpec((1, d), lambda j: (0, 0)),
            pl.BlockSpec(squeeze + (d, FFN_TF_FIRST), lambda j: lead + (0, j)),
            pl.BlockSpec(squeeze + (d, FFN_TF_FIRST), lambda j: lead + (0, j)),
            pl.BlockSpec(squeeze + (FFN_TF_FIRST, d), lambda j: lead + (j, 0)),
        ],
        out_specs=(
            pl.BlockSpec((tm, d), lambda j: (0, 0)),
            pl.BlockSpec((1, d, FFN_TF_FIRST), lambda j: (j, 0, 0)),
            pl.BlockSpec((1, d, FFN_TF_FIRST), lambda j: (j, 0, 0)),
            pl.BlockSpec((FFN_TF_FIRST, d), lambda j: (j, 0)),
        ),
        out_shape=(jax.ShapeDtypeStruct((tm, d), F32), jax.ShapeDtypeStruct((f // FFN_TF_FIRST, d, FFN_TF_FIRST), BF16),
                   jax.ShapeDtypeStruct((f // FFN_TF_FIRST, d, FFN_TF_FIRST), BF16), jax.ShapeDtypeStruct((f, d), BF16)),
        scratch_shapes=[pltpu.VMEM((tm, d), BF16)],
        compiler_params=_params("arbitrary"),
        name="ffn_first",
    )(x, gain.reshape(1, d), wg32, wu32, wd32)
    slabs = FFN_TF // FFN_TF_FIRST
    wcol = lambda i, j: (jnp.where(i > 0, j, 0), 0, 0)
    wrow = lambda i, j: (jnp.where(i > 0, j, 0), 0)
    return pl.pallas_call(
        _ffn_rest_body,
        grid=(s // tm, f // FFN_TF),
        in_specs=[
            pl.BlockSpec((tm, d), lambda i, j: (jnp.minimum(jnp.maximum(i, 1), s // tm - 1), 0)),
            pl.BlockSpec((FFN_COPY_ROWS, d), lambda i, j: (jnp.where(i == 0, jnp.minimum(j, n_copy - 1), n_copy - 1), 0)),
            pl.BlockSpec((1, d), lambda i, j: (0, 0)),
            pl.BlockSpec((slabs, d, FFN_TF_FIRST), wcol),
            pl.BlockSpec((slabs, d, FFN_TF_FIRST), wcol),
            pl.BlockSpec((FFN_TF, d), wrow),
        ],
        out_specs=pl.BlockSpec((tm, d), lambda i, j: (i, 0)),
        out_shape=jax.ShapeDtypeStruct((s, d), F32),
        scratch_shapes=[pltpu.VMEM((tm, d), BF16)],
        compiler_params=_params("arbitrary", "arbitrary", vmem_limit_bytes=FFN_REST_VMEM_LIMIT_BYTES),
        name="ffn_rest",
    )(x, y0, gain.reshape(1, d), wg, wu, wd)


def _nmm_body(*refs, n_extra, epilogue):
    x_ref, g_ref, w_ref = refs[:3]
    extra = refs[3:3 + n_extra]
    out_refs = refs[3 + n_extra:-1]
    h_ref = refs[-1]
    n = pl.program_id(1)

    @pl.when(n == 0)
    def _():
        x = x_ref[...]
        h_ref[...] = (x * _rms_scale(x) * g_ref[...]).astype(BF16)

    epilogue(n, h_ref, w_ref, extra, out_refs)


assert NMM_ROWS == KB


def _row_chunks(h_ref, w_ref, emit):
    w = w_ref[...].astype(BF16)
    for r in range(h_ref.shape[0] // NMM_ROWS):
        rows = slice(r * NMM_ROWS, (r + 1) * NMM_ROWS)
        emit(rows, jnp.dot(h_ref[rows, :], w, preferred_element_type=F32))


def _norm_matmul(x, gain, w, layer, n_tiles, extra, extra_specs, out_shapes, out_specs, epilogue, tn=NMM_TN,
                 tm=NMM_TM, x_single_buffer=False):
    s, d = x.shape
    tm = min(tm, s)
    last_whole = w.shape[-1] // tn - 1
    x_spec = (_resident if x_single_buffer else pl.BlockSpec)((tm, d), lambda i, n: (i, 0))
    return pl.pallas_call(
        functools.partial(_nmm_body, n_extra=len(extra), epilogue=epilogue),
        grid=(s // tm, n_tiles),
        in_specs=[
            x_spec,
            pl.BlockSpec((1, d), lambda i, n: (0, 0)),
            pl.BlockSpec((None, d, tn), lambda i, n: (layer, 0, jnp.minimum(n, last_whole))),
        ] + list(extra_specs),
        out_specs=out_specs,
        out_shape=out_shapes,
        scratch_shapes=[pltpu.VMEM((tm, d), BF16)],
        compiler_params=_params("parallel", "arbitrary"),
        name="norm_matmul",
    )(x, gain.reshape(1, d), w, *extra)


def _head_norm(y, gain, scale):
    parts = []
    for c in range(NMM_TN // A_HEAD_DIM):
        z = y[:, c * A_HEAD_DIM:(c + 1) * A_HEAD_DIM]
        parts.append(z * _rms_scale(z) * (gain * scale))
    return jnp.concatenate(parts, axis=1)


def _dsa_in_epilogue(n, h_ref, w_ref, extra, out_refs):
    qg_ref, kg_ref, ikg_ref, widx_ref = extra
    main_ref, iw_ref, vt_ref = out_refs
    q_tiles = A_Q_COLS // NMM_TN
    k_tile = q_tiles
    v_tile = k_tile + 1
    iq_tile0 = v_tile + 1
    idx_tile = A_MAIN_COLS // NMM_TN

    def store_main(fn):
        def emit(rows, y):
            main_ref[rows, :] = fn(y).astype(BF16)
        _row_chunks(h_ref, w_ref, emit)

    @pl.when(n < q_tiles)
    def _():
        store_main(lambda y: _head_norm(y, qg_ref[...], A_HEAD_DIM ** -0.5 * LOG2_E))

    @pl.when(n == k_tile)
    def _():
        store_main(lambda y: _head_norm(y, kg_ref[...], 1.0))

    @pl.when(n == v_tile)
    def _():
        tail = (lax.broadcasted_iota(I32, (VT_ROWS - A_HEAD_DIM, KB), 0) == 0).astype(BF16)

        def emit(rows, y):
            main_ref[rows, :] = y.astype(BF16)
            blk = rows.start // KB
            for j in range(A_KV_HEADS):
                vt_ref[blk, j * VT_ROWS:j * VT_ROWS + A_HEAD_DIM, :] = (
                    y[:, j * A_HEAD_DIM:(j + 1) * A_HEAD_DIM].T.astype(BF16))
                vt_ref[blk, j * VT_ROWS + A_HEAD_DIM:(j + 1) * VT_ROWS, :] = tail
        _row_chunks(h_ref, w_ref, emit)

    @pl.when(jnp.logical_and(n >= iq_tile0, n < idx_tile))
    def _():
        store_main(lambda y: y * IDX_DIM ** -0.5)

    @pl.when(n == idx_tile)
    def _():
        def emit(rows, z):
            lane = lax.broadcasted_iota(I32, z.shape, 1)
            ms = jnp.sum(jnp.where(lane < IDX_DIM, z * z, 0.0), axis=-1, keepdims=True) * (1.0 / IDX_DIM)
            ikn = z * lax.rsqrt(ms + EPS) * ikg_ref[...]
            main_ref[rows, :] = jnp.concatenate(
                [ikn, jnp.zeros((z.shape[0], NMM_TN - LANES), F32)], axis=1).astype(BF16)
            iw_ref[rows, :] = z * IDX_HEADS ** -0.5
        _row_chunks(h_ref, widx_ref, emit)


def _dsa_in_proj(x, gain, w_in, w_idx, layer, q_gain, k_gain, ik_gain_padded):
    s, d = x.shape
    tm = min(NMM_TM, s)
    n_tiles = A_MAIN_COLS // NMM_TN + 1
    ncols = n_tiles * NMM_TN
    vec = lambda i, n: (0, 0)
    return _norm_matmul(
        x, gain, w_in, layer, n_tiles,
        extra=(q_gain.reshape(1, A_HEAD_DIM), k_gain.reshape(1, A_HEAD_DIM), ik_gain_padded, w_idx),
        extra_specs=[pl.BlockSpec((1, A_HEAD_DIM), vec), pl.BlockSpec((1, A_HEAD_DIM), vec),
                     pl.BlockSpec((1, LANES), vec), pl.BlockSpec((None, d, LANES), lambda i, n: (layer, 0, 0))],
        out_shapes=(jax.ShapeDtypeStruct((s, ncols), BF16), jax.ShapeDtypeStruct((s, LANES), F32),
                    jax.ShapeDtypeStruct((s // KB, A_KV_HEADS * VT_ROWS, KB), BF16)),
        out_specs=(pl.BlockSpec((tm, NMM_TN), lambda i, n: (i, n)),
                   pl.BlockSpec((tm, LANES), lambda i, n: (i, 0)),
                   pl.BlockSpec((tm // KB, A_KV_HEADS * VT_ROWS, KB), lambda i, n: (i, 0, 0))),
        epilogue=_dsa_in_epilogue,
    )


def _identity_epilogue(n, h_ref, w_ref, extra, out_refs):
    def emit(rows, y):
        out_refs[0][rows, :] = y
    _row_chunks(h_ref, w_ref, emit)


def _hgrn_in_proj(x, gain, w, layer):
    s = x.shape[0]
    tm = min(HG_IN_TM, s)
    return _norm_matmul(
        x, gain, w, layer, w.shape[-1] // HG_IN_TN, extra=(), extra_specs=[],
        out_shapes=jax.ShapeDtypeStruct((s, w.shape[-1]), F32),
        out_specs=pl.BlockSpec((tm, HG_IN_TN), lambda i, n: (i, n)),
        epilogue=_identity_epilogue, tn=HG_IN_TN, tm=HG_IN_TM, x_single_buffer=True,
    )


def _out_proj_body(a_ref, w_ref, r_ref, o_ref, wb_ref):
    @pl.when(pl.program_id(0) == 0)
    def _():
        wb_ref[...] = w_ref[...].astype(BF16)

    o_ref[...] = r_ref[...] + jnp.dot(a_ref[...], wb_ref[...], preferred_element_type=F32)


def _out_proj(a, w, layer, res):
    s, k = a.shape
    d = w.shape[-1]
    tm = min(OUT_TM, s)
    return pl.pallas_call(
        _out_proj_body,
        grid=(s // tm,),
        in_specs=[
            pl.BlockSpec((tm, k), lambda i: (i, 0)),
            _resident((None, k, d), lambda i: (layer, 0, 0)),
            pl.BlockSpec((tm, d), lambda i: (i, 0)),
        ],
        out_specs=pl.BlockSpec((tm, d), lambda i: (i, 0)),
        out_shape=jax.ShapeDtypeStruct((s, d), F32),
        scratch_shapes=[pltpu.VMEM((k, d), BF16)],
        compiler_params=_params("arbitrary"),
        name="out_proj",
    )(a, w, res)


def _t5_bucket_table():
    exact = REL_BUCKETS // 2
    d = np.arange(2 * QB)
    df = np.maximum(d, 1).astype(np.float64)
    val = np.log(df / exact) / math.log(REL_MAX_DIST / exact) * (REL_BUCKETS - exact)
    big = d >= exact
    frac = np.abs(val[big] - np.round(val[big]))
    assert np.all((frac > 1e-4) | (frac == 0.0))
    large = np.minimum(exact + np.floor(val + 1e-9).astype(np.int64), REL_BUCKETS - 1)
    return np.where(d < exact, d, large).astype(np.int32)


def _near_bucket_tiles():
    table = _t5_bucket_table()
    r = np.arange(QB)[:, None]
    t = np.arange(QB)[None, :]
    tiles = [table[np.maximum(n * QB + t - r, 0)] for n in range(2)]
    return np.stack(tiles).astype(np.int32)


def _bias_body(rel_ref, bkt_ref, o_ref):
    j = pl.program_id(1)
    bkt = bkt_ref[0]
    for g in range(A_GROUP):
        h = j * A_GROUP + g
        far = rel_ref[REL_BUCKETS - 1, h]
        acc = jnp.zeros(bkt.shape, F32)
        for b in range(REL_BUCKETS):
            acc = jnp.where(bkt == b, (rel_ref[b, h] - far) * LOG2_E, acc)
        o_ref[0, 0, :, g * QB:(g + 1) * QB] = acc


def _near_bias(rel_bias):
    bkt = jnp.asarray(_near_bucket_tiles())
    return pl.pallas_call(
        _bias_body,
        grid=(2, A_KV_HEADS),
        in_specs=[
            pl.BlockSpec(memory_space=pltpu.SMEM),
            pl.BlockSpec((1, QB, QB), lambda n, j: (n, 0, 0)),
        ],
        out_specs=pl.BlockSpec((1, 1, QB, A_GROUP * QB), lambda n, j: (n, j, 0, 0)),
        out_shape=jax.ShapeDtypeStruct((2, A_KV_HEADS, QB, A_GROUP * QB), F32),
        compiler_params=_params("arbitrary", "arbitrary"),
        name="near_bias",
    )(rel_bias, bkt)


def _bit_transpose32(words):
    a = list(words)
    for j, mask in ((16, 0x0000FFFF), (8, 0x00FF00FF), (4, 0x0F0F0F0F), (2, 0x33333333), (1, 0x55555555)):
        shift = jnp.full(a[0].shape, j, I32)
        for k in range(32):
            if k & j == 0:
                t = (lax.shift_right_logical(a[k], shift) ^ a[k + j]) & mask
                a[k + j] = a[k + j] ^ t
                a[k] = a[k] ^ lax.shift_left(t, shift)
    return a


def _attn_body(q_ref, iq_ref, iw_ref, k_ref, ik_ref, vt_ref, bias_ref, o_ref,
               iqt_ref, qt_ref, sc_ref, plane_ref, eq_ref, hit_ref, st_ref, bmax_ref, acc_ref, m_ref, *, topk):
    i = pl.program_id(0)
    gq = A_GROUP * QB
    POS_BITS = (sc_ref.shape[0] - 1).bit_length()

    iqt = iq_ref[...].astype(F32).T
    iqt_ref[...] = jnp.zeros_like(iqt_ref)
    for h in range(IDX_HEADS):
        iqt_ref[0:IDX_DIM, h * QB:(h + 1) * QB] = iqt[h * IDX_DIM:(h + 1) * IDX_DIM, :].astype(BF16)
    qt = q_ref[...].astype(F32).T
    eye = (lax.broadcasted_iota(I32, (QB, QB), 0) == lax.broadcasted_iota(I32, (QB, QB), 1)).astype(BF16)
    for h in range(A_HEADS):
        qt_ref[0:A_HEAD_DIM, h * QB:(h + 1) * QB] = qt[h * A_HEAD_DIM:(h + 1) * A_HEAD_DIM, :].astype(BF16)
        qt_ref[A_HEAD_DIM:, h * QB:(h + 1) * QB] = eye
    iwt = iw_ref[...].T

    n_chunks = (i * QB + QB + SCORE_CHUNK - 1) // SCORE_CHUNK

    def score_chunk(c, causal_mask):
        row0 = pl.multiple_of(c * SCORE_CHUNK, SCORE_CHUNK)
        ikc = ik_ref[pl.ds(row0, SCORE_CHUNK), :]
        sc = jnp.zeros((SCORE_CHUNK, QB), F32)
        for hp in range(IDX_HEADS // 2):
            raw = jnp.dot(ikc, iqt_ref[:, hp * 2 * QB:(hp + 1) * 2 * QB], preferred_element_type=F32)
            for u in range(2):
                h = 2 * hp + u
                sc = sc + jnp.maximum(raw[:, u * QB:(u + 1) * QB], 0.0) * iwt[IDX_DIM + h:IDX_DIM + h + 1, :]
        bits = pltpu.bitcast(sc, I32)
        skey = jnp.where(bits >= 0, bits, bits ^ 0x7FFFFFFF)
        if causal_mask:
            key_pos = row0 + lax.broadcasted_iota(I32, sc.shape, 0)
            q_pos = i * QB + lax.broadcasted_iota(I32, sc.shape, 1)
            skey = jnp.where(key_pos <= q_pos, skey, INT_MIN)
        sc_ref[pl.ds(row0, SCORE_CHUNK), :] = skey
        ukey = skey ^ INT_MIN
        for gi in range(SCORE_CHUNK // PLANE_KEYS):
            base = gi * PLANE_KEYS
            planes = _bit_transpose32([ukey[base + SUBLANES * j:base + SUBLANES * (j + 1), :]
                                       for j in range(KEY_BITS)])
            for b in range(KEY_BITS):
                plane_ref[c * (SCORE_CHUNK // PLANE_KEYS) + gi, b] = planes[b]

    def full_chunk(c, carry):
        score_chunk(c, False)
        return carry

    lax.fori_loop(0, n_chunks - 1, full_chunk, 0)
    score_chunk(n_chunks - 1, True)

    eq_ref[...] = jnp.full_like(eq_ref, -1)
    hit_ref[...] = jnp.full_like(hit_ref, -1)
    groups_per_iter = SCORE_CHUNK // PLANE_KEYS

    def settle(g, keep_prev):
        hit = hit_ref[g]
        return jnp.where(keep_prev != 0, hit, eq_ref[g] ^ hit)

    def bit_pass(it, carry):
        chosen, n_greater, keep_prev = carry
        bit = KEY_BITS - 1 - it

        def sweep(gg, acc):
            for u in range(groups_per_iter):
                g = gg * groups_per_iter + u
                eq = settle(g, keep_prev)
                hit = eq & plane_ref[g, bit]
                eq_ref[g] = eq
                hit_ref[g] = hit
                acc = acc + lax.population_count(hit)
            return acc

        acc = lax.fori_loop(0, n_chunks, sweep, jnp.zeros((SUBLANES, QB), I32))
        n_ge = n_greater + jnp.sum(acc, axis=0, keepdims=True)
        keep = n_ge >= topk
        chosen = jnp.where(keep, chosen | (jnp.int32(1) << bit), chosen)
        n_greater = jnp.where(keep, n_greater, n_ge)
        return chosen, n_greater, keep.astype(I32)

    zeros_row = jnp.zeros((1, QB), I32)
    chosen, n_greater, keep_last = lax.fori_loop(0, KEY_BITS, bit_pass, (zeros_row, zeros_row, zeros_row + 1))
    thr = jnp.maximum(chosen ^ INT_MIN, INT_MIN + 1)

    def count_equal(gg, acc):
        for u in range(groups_per_iter):
            acc = acc + lax.population_count(settle(gg * groups_per_iter + u, keep_last))
        return acc

    n_equal = jnp.sum(lax.fori_loop(0, n_chunks, count_equal, jnp.zeros((SUBLANES, QB), I32)), axis=0, keepdims=True)
    n_ge_thr = jnp.where(chosen == 0, 0, n_greater + n_equal)

    n_count = n_chunks * (SCORE_CHUNK // COUNT_ROWS)

    def count_keys(pred):
        def body(r, acc):
            row0 = pl.multiple_of(r * COUNT_ROWS, COUNT_ROWS)
            hit = jnp.where(pred(sc_ref[pl.ds(row0, COUNT_ROWS), :], row0), 1, 0).astype(I32)
            return acc + jnp.sum(hit.reshape(COUNT_ROWS // SUBLANES, SUBLANES, QB), axis=0)
        acc = lax.fori_loop(0, n_count, body, jnp.zeros((SUBLANES, QB), I32))
        return jnp.sum(acc, axis=0, keepdims=True)

    @pl.when(jnp.max(n_ge_thr) > topk)
    def _():
        need = topk - n_greater

        def key_pos_of(blk, row0):
            return row0 + lax.broadcasted_iota(I32, blk.shape, 0)

        def pos_step(it, last):
            cand = last | (jnp.int32(1) << (POS_BITS - 1 - it))
            n_before = count_keys(lambda blk, row0: jnp.logical_and(blk == thr, key_pos_of(blk, row0) < cand))
            return jnp.where(n_before < need, cand, last)

        last = lax.fori_loop(0, POS_BITS, pos_step, jnp.zeros((1, QB), I32))

        def demote(r, carry):
            row0 = pl.multiple_of(r * COUNT_ROWS, COUNT_ROWS)
            blk = sc_ref[pl.ds(row0, COUNT_ROWS), :]
            late_tie = jnp.logical_and(blk == thr, key_pos_of(blk, row0) > last)
            sc_ref[pl.ds(row0, COUNT_ROWS), :] = jnp.where(late_tie, thr - 1, blk)
            return carry

        lax.fori_loop(0, n_count, demote, 0)

    m_ref[...] = jnp.full_like(m_ref, M_INIT)
    acc_ref[...] = jnp.zeros_like(acc_ref)
    heads = range(A_KV_HEADS)

    def logits(c, slot):
        row0 = pl.multiple_of(c * KB, KB)
        sel = sc_ref[pl.ds(row0, KB), :] >= thr
        mask_cols = jnp.where(sel, 0.0, MASKED_LOGIT).astype(BF16)
        kblk = k_ref[pl.ds(row0, KB), :]
        for j in heads:
            kaug = jnp.concatenate([kblk[:, j * A_HEAD_DIM:(j + 1) * A_HEAD_DIM], mask_cols], axis=1)
            st = jnp.dot(kaug, qt_ref[:, j * gq:(j + 1) * gq], preferred_element_type=F32)
            st_ref[slot, j] = st
            bmax_ref[slot, j] = jnp.max(st, axis=0, keepdims=True)

    def softmax_pv(c, slot, near):
        ps, alphas = [], []
        for j in heads:
            st = st_ref[slot, j]
            if near != (None, None):
                halves = [st[0:QB], st[QB:KB]]
                halves = [hv if n is None else hv + bias_ref[n, j] for hv, n in zip(halves, near)]
                st = jnp.concatenate(halves, axis=0)
                block_max = jnp.max(st, axis=0, keepdims=True)
            else:
                block_max = bmax_ref[slot, j]
            m_old = m_ref[j]
            m_new = jnp.maximum(m_old, block_max)
            m_ref[j] = m_new
            ps.append(jnp.exp2(st - m_new).astype(BF16))
            alphas.append(jnp.exp2(m_old - m_new))
        vblk = vt_ref[c]
        pvs = [jnp.dot(vblk[j * VT_ROWS:(j + 1) * VT_ROWS, :], ps[j], preferred_element_type=F32)
               for j in heads]
        for j in heads:
            acc_ref[j] = acc_ref[j] * alphas[j] + pvs[j]

    far = (None, None)
    n_far = jnp.maximum(lax.shift_right_arithmetic(i - 1, 1), 0)
    i_odd = (i & 1) == 1

    @pl.when(i >= 1)
    def _():
        logits(0, 0)

    @pl.when(i == 0)
    def _():
        logits(0, 1)

    def far_pair(pp, carry):
        c0 = 2 * pp
        logits(c0 + 1, 1)
        softmax_pv(c0, 0, far)
        logits(c0 + 2, 0)
        softmax_pv(c0 + 1, 1, far)
        return carry

    lax.fori_loop(0, lax.shift_right_arithmetic(n_far, 1), far_pair, 0)

    @pl.when((n_far & 1) == 1)
    def _():
        softmax_pv(n_far - 1, 0, far)
        logits(n_far, 0)

    @pl.when(i_odd)
    def _():
        softmax_pv(n_far, 0, (1, 0))

    @pl.when(jnp.logical_and(jnp.logical_not(i_odd), i >= 2))
    def _():
        logits(n_far + 1, 1)
        softmax_pv(n_far, 0, (None, 1))

    @pl.when(jnp.logical_not(i_odd))
    def _():
        softmax_pv(lax.shift_right_arithmetic(i, 1), 1, (0, None))

    for j in heads:
        acc = acc_ref[j]
        out_t = acc[0:A_HEAD_DIM, :] * (1.0 / acc[A_HEAD_DIM:A_HEAD_DIM + 1, :])
        for g in range(A_GROUP):
            h = j * A_GROUP + g
            o_ref[:, h * A_HEAD_DIM:(h + 1) * A_HEAD_DIM] = out_t[:, g * QB:(g + 1) * QB].T.astype(o_ref.dtype)


def _dsa_attention(main, iw, vt3, bias, topk):
    s = main.shape[0]
    nb = s // QB
    return pl.pallas_call(
        functools.partial(_attn_body, topk=topk),
        grid=(nb,),
        in_specs=[
            pl.BlockSpec((QB, A_Q_COLS), lambda i: (i, 0)),
            pl.BlockSpec((QB, A_IQ_COLS), lambda i: (i, (A_Q_COLS + 2 * A_KV_COLS) // A_IQ_COLS)),
            pl.BlockSpec((QB, LANES), lambda i: (i, 0)),
            _resident((s, A_KV_COLS), lambda i: (0, A_Q_COLS // A_KV_COLS)),
            _resident((s, LANES), lambda i: (0, A_MAIN_COLS // LANES)),
            _resident((s // KB, A_KV_HEADS * VT_ROWS, KB), lambda i: (0, 0, 0)),
            _resident((2, A_KV_HEADS, QB, A_GROUP * QB), lambda i: (0, 0, 0, 0)),
        ],
        out_specs=pl.BlockSpec((QB, A_Q_COLS), lambda i: (i, 0)),
        out_shape=jax.ShapeDtypeStruct((s, A_Q_COLS), BF16),
        scratch_shapes=[
            pltpu.VMEM((LANES, IDX_HEADS * QB), BF16),
            pltpu.VMEM((A_HEAD_DIM + QB, A_HEADS * QB), BF16),
            pltpu.VMEM((s, QB), I32),
            pltpu.VMEM((s // PLANE_KEYS, KEY_BITS, SUBLANES, QB), I32),
            pltpu.VMEM((s // PLANE_KEYS, SUBLANES, QB), I32),
            pltpu.VMEM((s // PLANE_KEYS, SUBLANES, QB), I32),
            pltpu.VMEM((2, A_KV_HEADS, KB, A_GROUP * QB), F32),
            pltpu.VMEM((2, A_KV_HEADS, 1, A_GROUP * QB), F32),
            pltpu.VMEM((A_KV_HEADS, VT_ROWS, A_GROUP * QB), F32),
            pltpu.VMEM((A_KV_HEADS, 1, A_GROUP * QB), F32),
        ],
        compiler_params=_params("arbitrary"),
        name="dsa_attention",
    )(main, main, iw, main, main, vt3, bias)


def _dsa_mixer(x, gain, w_in, w_idx, w_out, layer, q_gain, k_gain, ik_gain_padded, bias):
    s = x.shape[0]
    topk = min(TOPK_MAX, s // 4)
    main, iw, vt3 = _dsa_in_proj(x, gain, w_in, w_idx, layer, q_gain, k_gain, ik_gain_padded)
    o = _dsa_attention(main, iw, vt3, bias, topk)
    return _out_proj(o, w_out, layer, x)


def _hgrn_sum_matrices():
    c = HG_CHUNK
    t = np.arange(c)[:, None]
    s = np.arange(c)[None, :]
    mats = []
    for l in range(1, HG_LEVELS + 1):
        blk = 1 << l
        m = (t // blk) * blk + blk // 2
        upper = t >= m
        mats.append(np.where(upper, (s >= m) & (s <= t), (s > t) & (s <= m - 1)))
    mats.append(s <= t)
    mats.append(s > t)
    return np.concatenate(mats, axis=0).astype(np.float32)


def _hgrn_split_levels():
    t = np.arange(HG_CHUNK)[:, None]
    s = np.arange(HG_CHUNK)[None, :]
    lvl = np.zeros((HG_CHUNK, HG_CHUNK), np.int32)
    for l in range(1, HG_LEVELS + 1):
        lvl[(s < t) & ((t >> l) == (s >> l)) & (((t >> (l - 1)) & 1) == 1) & (((s >> (l - 1)) & 1) == 0)] = l
    assert np.all((lvl > 0) == (s < t))
    return lvl


def _hgrn_body(q_ref, f_ref, i_ref, g_ref, lbl_ref, og_ref, gm_ref, split_ref, o_ref, st_ref, *, layer):
    c = pl.program_id(1)
    ch = HG_CHUNK

    @pl.when(c == 0)
    def _():
        st_ref[...] = jnp.zeros_like(st_ref)

    logits = lbl_ref[...]
    e = jnp.exp(logits - jnp.max(logits, axis=0, keepdims=True))
    p = e / jnp.sum(e, axis=0, keepdims=True)
    csum = p[0:1, :]
    for r in range(1, layer + 1):
        csum = csum + p[r:r + 1, :]
    lb = csum - p[0:1, :]

    def chunk(rows):
        fgate = lb + (1.0 - lb) * jax.nn.sigmoid(f_ref[rows, :])
        kin = 1.0 - fgate
        logf = jnp.log2(fgate)
        logf_hi = logf.astype(BF16)
        logf_lo = (logf - logf_hi.astype(F32)).astype(BF16)
        logf_split = jnp.concatenate([logf_hi, logf_lo], axis=0)
        decs = [jnp.dot(gm_ref[r * ch:(r + 1) * ch, :], logf_split, preferred_element_type=F32)
                for r in range(HG_LEVELS + 2)]

        row = lax.broadcasted_iota(I32, (ch, B_KEY_DIM), 0)
        split = split_ref[...]
        nt = (((1,), (1,)), ((), ()))
        heads = range(HG_HEADS_PER_STEP)
        levels = range(1, HG_LEVELS + 1)
        sls = [slice(hh * B_KEY_DIM, (hh + 1) * B_KEY_DIM) for hh in heads]

        qs = [q_ref[rows, sl] * (B_KEY_DIM ** -0.5) for sl in sls]
        ks = [kin[:, sl] for sl in sls]
        vs = [i_ref[rows, sl] for sl in sls]
        b_incl = [decs[HG_LEVELS][:, sl] for sl in sls]
        b_rest = [decs[HG_LEVELS + 1][:, sl] for sl in sls]
        states = [st_ref[hh] for hh in heads]

        zs = []
        for hh in heads:
            for l in levels:
                upper = (row & (1 << (l - 1))) != 0
                ex = jnp.exp2(decs[l - 1][:, sls[hh]])
                zs.append((jnp.where(upper, qs[hh], ks[hh]) * ex).astype(BF16))
        prods = [lax.dot_general(z, z, nt, preferred_element_type=F32) for z in zs]
        o_inter = [lax.dot_general((qs[hh] * jnp.exp2(b_incl[hh])).astype(BF16), states[hh].astype(BF16), nt,
                                   preferred_element_type=F32) for hh in heads]
        new_kv = [jnp.dot(vs[hh].T.astype(BF16), (ks[hh] * jnp.exp2(b_rest[hh])).astype(BF16),
                          preferred_element_type=F32) for hh in heads]
        a_mats = []
        for hh in heads:
            a = jnp.zeros((ch, ch), F32)
            for l in levels:
                a = jnp.where(split == l, prods[hh * HG_LEVELS + l - 1], a)
            a_mats.append(a.astype(BF16))
        o_intra = [jnp.dot(a_mats[hh], vs[hh].astype(BF16), preferred_element_type=F32) for hh in heads]

        for hh in heads:
            sl = sls[hh]
            st_ref[hh] = states[hh] * jnp.exp2(b_incl[hh][ch - 1:ch, :]) + new_kv[hh]
            o = o_inter[hh] + o_intra[hh] + jnp.sum(qs[hh] * ks[hh], axis=-1, keepdims=True) * vs[hh]
            on = o * _rms_scale(o) * og_ref[:, sl]
            gate = g_ref[rows, sl]
            o_ref[rows, sl] = (on * (gate * jax.nn.sigmoid(gate))).astype(o_ref.dtype)

    for cc in range(HG_CHUNKS_PER_STEP):
        chunk(slice(cc * ch, (cc + 1) * ch))


def _hgrn_core(proj, lb_logits, o_gain, layer):
    s = proj.shape[0]
    w = HG_HEADS_PER_STEP * B_KEY_DIM
    nhb = D_MODEL // w
    rows_per_step = HG_CHUNKS_PER_STEP * HG_CHUNK
    gm = jnp.asarray(np.tile(_hgrn_sum_matrices(), (1, 2)), dtype=BF16)
    split = jnp.asarray(_hgrn_split_levels())
    col = lambda off: (lambda hb, c: (c, off * nhb + hb))
    return pl.pallas_call(
        functools.partial(_hgrn_body, layer=layer),
        grid=(nhb, s // rows_per_step),
        in_specs=[
            pl.BlockSpec((rows_per_step, w), col(0)),
            pl.BlockSpec((rows_per_step, w), col(1)),
            pl.BlockSpec((rows_per_step, w), col(2)),
            pl.BlockSpec((rows_per_step, w), col(3)),
            pl.BlockSpec((DEPTH, w), lambda hb, c: (0, hb)),
            pl.BlockSpec((1, w), lambda hb, c: (0, hb)),
            pl.BlockSpec(gm.shape, lambda hb, c: (0, 0)),
            pl.BlockSpec(split.shape, lambda hb, c: (0, 0)),
        ],
        out_specs=pl.BlockSpec((rows_per_step, w), lambda hb, c: (c, hb)),
        out_shape=jax.ShapeDtypeStruct((s, D_MODEL), BF16),
        scratch_shapes=[pltpu.VMEM((HG_HEADS_PER_STEP, B_KEY_DIM, B_KEY_DIM), F32)],
        compiler_params=_params("parallel", "arbitrary"),
        name="hgrn_core",
    )(proj, proj, proj, proj, lb_logits, o_gain.reshape(1, D_MODEL), gm, split)


def _hgrn_mixer(x, gain, w_in, w_out, mixer_idx, lb_logits, o_gain, layer):
    proj = _hgrn_in_proj(x, gain, w_in, mixer_idx)
    o = _hgrn_core(proj, lb_logits, o_gain, layer)
    return _out_proj(o, w_out, mixer_idx, x)


def kernel(x, norm_gains, ffn_w_gate, ffn_w_up, ffn_w_down, dsa_w_in, dsa_w_out, dsa_q_gain, dsa_k_gain,
           dsa_idx_k_gain, rel_bias, hgrn_w_in, hgrn_w_out, hgrn_lb_logits, hgrn_o_gain):
    assert x.shape[0] == 1 and x.shape[2] == D_MODEL and x.shape[1] % FFN_TM == 0
    h = x.reshape(x.shape[1], D_MODEL)
    dsa_w_idx = jnp.pad(dsa_w_in[:, :, A_MAIN_COLS:], ((0, 0), (0, 0), (0, LANES - (A_IN_COLS - A_MAIN_COLS))))
    ik_gain = jnp.pad(dsa_idx_k_gain, ((0, 0), (0, LANES - IDX_DIM)))
    bias = _near_bias(rel_bias)

    for layer in range(DEPTH):
        j = layer // N_MIXERS
        h = _ffn(h, norm_gains[layer, 0], ffn_w_gate, ffn_w_up, ffn_w_down, (layer, 0))
        if layer % N_MIXERS == 0:
            h = _dsa_mixer(h, norm_gains[layer, 1], dsa_w_in, dsa_w_idx, dsa_w_out, j, dsa_q_gain[j], dsa_k_gain[j],
                           ik_gain[j:j + 1], bias)
        else:
            h = _hgrn_mixer(h, norm_gains[layer, 1], hgrn_w_in, hgrn_w_out, j, hgrn_lb_logits, hgrn_o_gain[j], layer)
        h = _ffn(h, norm_gains[layer, 2], ffn_w_gate, ffn_w_up, ffn_w_down, (layer, 1))
    return h.reshape(x.shape)
```
